```python
import math
import jax, jax.numpy as jnp
from jax import lax
import numpy as np

D_MODEL = 2048
BATCH = 4
SEQ = 4096
DEPTH = 4
DEC_BATCH = 2
DEC_SEQ = 8192
PAST_LEN = 128

S5_WIDTH = D_MODEL // 2
S5_GROUP = 16
S5_GROUPS = S5_WIDTH // S5_GROUP
S5_STATE = 64
MLA_HEADS = 16
QK_NOPE = 128
QK_ROPE = 64
V_HEAD = 128
Q_LORA = 512
KV_LORA = 512
ROPE_BASE = 10000.0
Q_BLOCK = 128
N_EXPERTS = 16
EC_CAPACITY = 2
D_EXPERT = 1408
DEEPNORM_ALPHA = (2 * DEPTH) ** 0.25
DEEPNORM_BETA = (8 * DEPTH) ** -0.25
LN_EPS = 1e-5
RMS_EPS = 1e-6
IN_SPLITS = [S5_WIDTH, S5_WIDTH + Q_LORA, S5_WIDTH + Q_LORA + KV_LORA, S5_WIDTH + Q_LORA + KV_LORA + QK_ROPE, S5_WIDTH + Q_LORA + KV_LORA + QK_ROPE + D_MODEL]
IN_WIDTH = S5_WIDTH + Q_LORA + KV_LORA + QK_ROPE + 2 * D_MODEL

kernel_name = 'hybrid_s5_mla_ec_encoder'


def layer_norm(x, g, b):
    xf = x.astype(jnp.float32)
    mu = xf.mean(-1, keepdims=True)
    var = jnp.square(xf - mu).mean(-1, keepdims=True)
    return ((xf - mu) * lax.rsqrt(var + LN_EPS) * g.astype(jnp.float32) + b.astype(jnp.float32)).astype(x.dtype)


def rms_norm(x, g):
    xf = x.astype(jnp.float32)
    return (xf * lax.rsqrt(jnp.mean(xf * xf, -1, keepdims=True) + RMS_EPS) * g.astype(jnp.float32)).astype(x.dtype)


def rope_tables(length, dtype):
    pos = jnp.arange(length, dtype=jnp.float32)
    inv = 1.0 / (ROPE_BASE ** (jnp.arange(0, QK_ROPE, 2, dtype=jnp.float32) / QK_ROPE))
    ang = pos[:, None] * inv[None, :]
    return jnp.cos(ang).astype(dtype), jnp.sin(ang).astype(dtype)


def apply_rope(x, cos, sin):
    x1, x2 = jnp.split(x, 2, axis=-1)
    return jnp.concatenate([x1 * cos - x2 * sin, x2 * cos + x1 * sin], axis=-1)


def _linear_recurrence_op(e1, e2):
    a1r, a1i, b1r, b1i = e1
    a2r, a2i, b2r, b2i = e2
    return (a2r * a1r - a2i * a1i,
            a2r * a1i + a2i * a1r,
            a2r * b1r - a2i * b1i + b2r,
            a2r * b1i + a2i * b1r + b2i)


def s5_mixer(u, lam_re, lam_im, log_step, b_re, b_im, c_re, c_im, d_skip, w_glu, b_glu):
    bt, length, _ = u.shape
    ug = u.reshape(bt, length, S5_GROUPS, S5_GROUP)
    y = d_skip * u
    for direction in range(2):
        lr = lam_re[direction].astype(jnp.float32)
        li = lam_im[direction].astype(jnp.float32)
        step = jnp.exp(log_step[direction].astype(jnp.float32))[:, None]
        mag = jnp.exp(lr * step)
        ar = mag * jnp.cos(li * step)
        ai = mag * jnp.sin(li * step)
        den = lr * lr + li * li
        fr = ((ar - 1.0) * lr + ai * li) / den
        fi = (ai * lr - (ar - 1.0) * li) / den
        br = b_re[direction].astype(jnp.float32)
        bi = b_im[direction].astype(jnp.float32)
        bbar_r = (fr[..., None] * br - fi[..., None] * bi).astype(u.dtype)
        bbar_i = (fr[..., None] * bi + fi[..., None] * br).astype(u.dtype)
        bu_r = jnp.einsum('blgh,gph->blgp', ug, bbar_r)
        bu_i = jnp.einsum('blgh,gph->blgp', ug, bbar_i)
        a_r = jnp.broadcast_to(ar.astype(u.dtype), bu_r.shape)
        a_i = jnp.broadcast_to(ai.astype(u.dtype), bu_i.shape)
        _, _, s_r, s_i = lax.associative_scan(_linear_recurrence_op, (a_r, a_i, bu_r, bu_i),
                                              reverse=(direction == 1), axis=1)
        y_dir = (jnp.einsum('blgp,ghp->blgh', s_r, c_re[direction])
                 - jnp.einsum('blgp,ghp->blgh', s_i, c_im[direction]))
        y = y + y_dir.reshape(bt, length, S5_WIDTH)
    act = jax.nn.gelu(y)
    return act * jax.nn.sigmoid(act @ w_glu + b_glu)


def mla_mixer(c_q, c_kv, k_r, q_norm_g, w_uq, kv_norm_g, w_ukv):
    bt, length, _ = c_q.shape
    cos, sin = rope_tables(length, c_q.dtype)
    q = (rms_norm(c_q, q_norm_g) @ w_uq).reshape(bt, length, MLA_HEADS, QK_NOPE + QK_ROPE)
    q_nope = q[..., :QK_NOPE]
    q_rope = apply_rope(q[..., QK_NOPE:], cos[:, None, :], sin[:, None, :])
    kv = (rms_norm(c_kv, kv_norm_g) @ w_ukv).reshape(bt, length, MLA_HEADS, QK_NOPE + V_HEAD)
    k_nope = kv[..., :QK_NOPE]
    v = kv[..., QK_NOPE:]
    k_rope = apply_rope(k_r, cos, sin)
    scale = (QK_NOPE + QK_ROPE) ** -0.5
    n_blk = length // Q_BLOCK

    def to_blocks(t):
        return t.reshape(bt, n_blk, Q_BLOCK, MLA_HEADS, t.shape[-1]).transpose(1, 0, 2, 3, 4)

    def attend(blk):
        qn, qr = blk
        s = (jnp.einsum('bqhd,bkhd->bhqk', qn, k_nope)
             + jnp.einsum('bqhr,bkr->bhqk', qr, k_rope)).astype(jnp.float32) * scale
        p = jax.nn.softmax(s, axis=-1).astype(v.dtype)
        return jnp.einsum('bhqk,bkhd->bqhd', p, v)

    o = lax.map(attend, (to_blocks(q_nope), to_blocks(q_rope)))
    return o.transpose(1, 0, 2, 3, 4).reshape(bt, length, MLA_HEADS * V_HEAD)


def expert_choice_moe(h, w_router, w1, w3, w2):
    n, d = h.shape
    cap = EC_CAPACITY * n // N_EXPERTS
    aff = jax.nn.softmax((h @ w_router).astype(jnp.float32), axis=-1)
    gate_w, tok_idx = lax.top_k(aff.T, cap)
    xe = jnp.take(h, tok_idx, axis=0)
    hid = jax.nn.silu(jnp.einsum('ecd,edf->ecf', xe, w1)) * jnp.einsum('ecd,edf->ecf', xe, w3)
    ye = jnp.einsum('ecf,efd->ecd', hid, w2) * gate_w[..., None].astype(h.dtype)
    return jnp.zeros_like(h).at[tok_idx.reshape(-1)].add(ye.reshape(-1, d))


def run_trunk(x, c, p):
    bt, length, d = x.shape
    x = layer_norm(x, p['ln_in_g'], p['ln_in_b'])
    cond = jax.nn.silu(c)
    for l in range(DEPTH):
        mod = (cond @ p['w_ada'][l] + p['b_ada'][l])[:, None, :]
        sh1, sc1, g1, sh2, sc2, g2 = jnp.split(mod, 6, axis=-1)
        h = x * (1 + sc1) + sh1
        proj = h @ p['w_in'][l]
        u, cq, ckv, kr, gate_s5, gate_mla = jnp.split(proj, IN_SPLITS, axis=-1)
        y_s5 = s5_mixer(u, p['s5_lam_re'][l], p['s5_lam_im'][l], p['s5_log_step'][l],
                        p['s5_b_re'][l], p['s5_b_im'][l], p['s5_c_re'][l], p['s5_c_im'][l],
                        p['s5_d'][l], p['s5_w_glu'][l], p['s5_b_glu'][l]) @ p['w_s5_proj'][l]
        y_mla = mla_mixer(cq, ckv, kr, p['mla_q_norm'][l], p['mla_w_uq'][l],
                          p['mla_kv_norm'][l], p['mla_w_ukv'][l]) @ p['w_mla_proj'][l]
        merged = jax.nn.sigmoid(gate_s5) * y_s5 + jax.nn.sigmoid(gate_mla) * y_mla
        out = merged @ p['w_o'][l]
        x = layer_norm(DEEPNORM_ALPHA * x + g1 * out, p['ln1_g'][l], p['ln1_b'][l])
        h2 = (x * (1 + sc2) + sh2).reshape(bt * length, d)
        moe = expert_choice_moe(h2, p['w_router'][l], p['w_exp1'][l], p['w_exp3'][l],
                                p['w_exp2'][l]).reshape(bt, length, d)
        x = layer_norm(DEEPNORM_ALPHA * x + g2 * moe, p['ln2_g'][l], p['ln2_b'][l])
    return x


def setup_inputs(seed: int = 0) -> dict:
    key = jax.random.key(seed)
    ks = jax.random.split(key, 40)
    f32 = jnp.float32
    D = D_MODEL

    def nrm(k, shape, scale):
        return jax.random.normal(k, shape, f32) * scale

    lam_im_base = math.pi * jnp.arange(S5_STATE, dtype=f32)
    return {
        'x_prompt': nrm(ks[0], (BATCH, SEQ, D), 1.0),
        'x_sample': nrm(ks[1], (DEC_BATCH, DEC_SEQ, D), 1.0),
        'c_prompt': nrm(ks[2], (BATCH, D), 1.0),
        'c_sample': nrm(ks[3], (DEC_BATCH, D), 1.0),
        'ln_in_g': 1.0 + nrm(ks[4], (D,), 0.01),
        'ln_in_b': nrm(ks[5], (D,), 0.01),
        'w_ada': nrm(ks[6], (DEPTH, D, 6 * D), 0.5 * D ** -0.5),
        'b_ada': nrm(ks[7], (DEPTH, 6 * D), 0.01),
        'w_in': nrm(ks[8], (DEPTH, D, IN_WIDTH), D ** -0.5),
        's5_lam_re': -0.5 + nrm(ks[9], (DEPTH, 2, S5_GROUPS, S5_STATE), 0.01),
        's5_lam_im': lam_im_base + nrm(ks[10], (DEPTH, 2, S5_GROUPS, S5_STATE), 0.01),
        's5_log_step': jax.random.uniform(ks[11], (DEPTH, 2, S5_GROUPS), f32, math.log(1e-3), math.log(1e-1)),
        's5_b_re': nrm(ks[12], (DEPTH, 2, S5_GROUPS, S5_STATE, S5_GROUP), (2 * S5_GROUP) ** -0.5),
        's5_b_im': nrm(ks[13], (DEPTH, 2, S5_GROUPS, S5_STATE, S5_GROUP), (2 * S5_GROUP) ** -0.5),
        's5_c_re': nrm(ks[14], (DEPTH, 2, S5_GROUPS, S5_GROUP, S5_STATE), (2 * S5_STATE) ** -0.5),
        's5_c_im': nrm(ks[15], (DEPTH, 2, S5_GROUPS, S5_GROUP, S5_STATE), (2 * S5_STATE) ** -0.5),
        's5_d': nrm(ks[16], (DEPTH, S5_WIDTH), 1.0),
        's5_w_glu': nrm(ks[17], (DEPTH, S5_WIDTH, S5_WIDTH), S5_WIDTH ** -0.5),
        's5_b_glu': nrm(ks[18], (DEPTH, S5_WIDTH), 0.01),
        'w_s5_proj': nrm(ks[19], (DEPTH, S5_WIDTH, D), S5_WIDTH ** -0.5),
        'mla_q_norm': 1.0 + nrm(ks[20], (DEPTH, Q_LORA), 0.01),
        'mla_w_uq': nrm(ks[21], (DEPTH, Q_LORA, MLA_HEADS * (QK_NOPE + QK_ROPE)), Q_LORA ** -0.5),
        'mla_kv_norm': 1.0 + nrm(ks[22], (DEPTH, KV_LORA), 0.01),
        'mla_w_ukv': nrm(ks[23], (DEPTH, KV_LORA, MLA_HEADS * (QK_NOPE + V_HEAD)), KV_LORA ** -0.5),
        'w_mla_proj': nrm(ks[24], (DEPTH, MLA_HEADS * V_HEAD, D), (MLA_HEADS * V_HEAD) ** -0.5),
        'w_o': nrm(ks[25], (DEPTH, D, D), DEEPNORM_BETA * D ** -0.5),
        'ln1_g': 1.0 + nrm(ks[26], (DEPTH, D), 0.01),
        'ln1_b': nrm(ks[27], (DEPTH, D), 0.01),
        'w_router': nrm(ks[28], (DEPTH, D, N_EXPERTS), D ** -0.5),
        'w_exp1': nrm(ks[29], (DEPTH, N_EXPERTS, D, D_EXPERT), D ** -0.5),
        'w_exp3': nrm(ks[30], (DEPTH, N_EXPERTS, D, D_EXPERT), D ** -0.5),
        'w_exp2': nrm(ks[31], (DEPTH, N_EXPERTS, D_EXPERT, D), DEEPNORM_BETA * D_EXPERT ** -0.5),
        'ln2_g': 1.0 + nrm(ks[32], (DEPTH, D), 0.01),
        'ln2_b': nrm(ks[33], (DEPTH, D), 0.01),
    }


def reference(x_prompt, x_sample, c_prompt, c_sample, ln_in_g, ln_in_b, w_ada, b_ada, w_in,
              s5_lam_re, s5_lam_im, s5_log_step, s5_b_re, s5_b_im, s5_c_re, s5_c_im, s5_d,
              s5_w_glu, s5_b_glu, w_s5_proj, mla_q_norm, mla_w_uq, mla_kv_norm, mla_w_ukv,
              w_mla_proj, w_o, ln1_g, ln1_b, w_router, w_exp1, w_exp3, w_exp2, ln2_g, ln2_b):
    params = dict(ln_in_g=ln_in_g, ln_in_b=ln_in_b, w_ada=w_ada, b_ada=b_ada, w_in=w_in,
                  s5_lam_re=s5_lam_re, s5_lam_im=s5_lam_im, s5_log_step=s5_log_step,
                  s5_b_re=s5_b_re, s5_b_im=s5_b_im, s5_c_re=s5_c_re, s5_c_im=s5_c_im, s5_d=s5_d,
                  s5_w_glu=s5_w_glu, s5_b_glu=s5_b_glu, w_s5_proj=w_s5_proj,
                  mla_q_norm=mla_q_norm, mla_w_uq=mla_w_uq, mla_kv_norm=mla_kv_norm,
                  mla_w_ukv=mla_w_ukv, w_mla_proj=w_mla_proj, w_o=w_o, ln1_g=ln1_g, ln1_b=ln1_b,
                  w_router=w_router, w_exp1=w_exp1, w_exp3=w_exp3, w_exp2=w_exp2,
                  ln2_g=ln2_g, ln2_b=ln2_b)
    y_prompt = run_trunk(x_prompt, c_prompt, params)
    y_sample = run_trunk(x_sample, c_sample, params)
    return (y_prompt, y_sample)
```

```python
import functools
import math

import jax
import jax.numpy as jnp
from jax import lax
from jax.experimental import pallas as pl
from jax.experimental.pallas import tpu as pltpu

F32 = jnp.float32
BF16 = jnp.bfloat16

D_MODEL = 2048
DEPTH = 4
S5_WIDTH = D_MODEL // 2
S5_GROUP = 16
S5_GROUPS = S5_WIDTH // S5_GROUP
S5_STATE = 64
MLA_HEADS = 16
QK_NOPE = 128
QK_ROPE = 64
V_HEAD = 128
Q_LORA = 512
KV_LORA = 512
ROPE_BASE = 10000.0
N_EXPERTS = 16
EC_CAPACITY = 2
D_EXPERT = 1408
DEEPNORM_ALPHA = (2 * DEPTH) ** 0.25
LN_EPS = 1e-5
RMS_EPS = 1e-6

LANES = 128
SUBLANES = 8
MXU_DIM = 256
S5_CHUNK = MXU_DIM // S5_GROUP
HEAD_PAD = 2 * LANES
ROPE_HALF = QK_ROPE // 2
COL_U, COL_CQ, COL_CKV = 0, S5_WIDTH, S5_WIDTH + Q_LORA
COL_GS = S5_WIDTH + Q_LORA + KV_LORA
COL_GM = COL_GS + D_MODEL
COL_KR = COL_GM + D_MODEL
PROJ_W = COL_KR + LANES
MOE_SLAB = 64
SLOT_ALIGN = 16
VMEM_LIMIT = 56 * 1024 * 1024


def _cparams(sem, vmem=VMEM_LIMIT):
    return pltpu.CompilerParams(dimension_semantics=sem, vmem_limit_bytes=vmem)


def _dot(a, b):
    return jnp.dot(a, b, preferred_element_type=F32)


def _layer_norm(y, g, b):
    mu = jnp.mean(y, axis=-1, keepdims=True)
    yc = y - mu
    var = jnp.mean(yc * yc, axis=-1, keepdims=True)
    return yc * lax.rsqrt(var + LN_EPS) * g + b


def _sigmoid(x):
    return 1.0 / (1.0 + jnp.exp(-x))


def _ln_in_kernel(x_ref, g_ref, b_ref, o_ref):
    o_ref[...] = _layer_norm(x_ref[...], g_ref[...], b_ref[...])


def _ln_in(x, g, b, bm):
    n, d = x.shape
    return pl.pallas_call(
        _ln_in_kernel,
        grid=(n // bm,),
        in_specs=[pl.BlockSpec((bm, d), lambda i: (i, 0)),
                  pl.BlockSpec((1, d), lambda i: (0, 0)),
                  pl.BlockSpec((1, d), lambda i: (0, 0))],
        out_specs=pl.BlockSpec((bm, d), lambda i: (i, 0)),
        out_shape=jax.ShapeDtypeStruct((n, d), F32),
        compiler_params=_cparams(("parallel",)),
    )(x, g.reshape(1, d), b.reshape(1, d))


def _ada_kernel(c_ref, w_ref, b_ref, o_ref):
    c = c_ref[...]
    cond = (c * _sigmoid(c)).astype(BF16)
    o_ref[0] = _dot(cond, w_ref[0].astype(BF16)) + b_ref[0]


def _ada(c_pad, w_ada, b_ada, bn=1024):
    depth, d, n6 = w_ada.shape
    rows = c_pad.shape[0]
    return pl.pallas_call(
        _ada_kernel,
        grid=(depth, n6 // bn),
        in_specs=[pl.BlockSpec((rows, d), lambda l, j: (0, 0)),
                  pl.BlockSpec((1, d, bn), lambda l, j: (l, 0, j)),
                  pl.BlockSpec((1, 1, bn), lambda l, j: (l, 0, j))],
        out_specs=pl.BlockSpec((1, rows, bn), lambda l, j: (l, 0, j)),
        out_shape=jax.ShapeDtypeStruct((depth, rows, n6), F32),
        compiler_params=_cparams(("parallel", "parallel")),
    )(c_pad, w_ada, b_ada.reshape(depth, 1, n6))


def _inproj_kernel(x_ref, mod_ref, w_ref, o_ref, h_ref):
    @pl.when(pl.program_id(1) == 0)
    def _():
        m = mod_ref[0]
        h_ref[...] = (x_ref[...] * (1.0 + m[1:2]) + m[0:1]).astype(BF16)

    o_ref[...] = _dot(h_ref[...], w_ref[...]).astype(o_ref.dtype)


def _inproj(x, mod, w, bm, bn, seg):
    n, d = x.shape
    nw = w.shape[1]
    return pl.pallas_call(
        _inproj_kernel,
        grid=(n // bm, nw // bn),
        in_specs=[pl.BlockSpec((bm, d), lambda i, j: (i, 0)),
                  pl.BlockSpec((1, 6, d), lambda i, j: ((i * bm) // seg, 0, 0)),
                  pl.BlockSpec((d, bn), lambda i, j: (0, j))],
        out_specs=pl.BlockSpec((bm, bn), lambda i, j: (i, j)),
        out_shape=jax.ShapeDtypeStruct((n, nw), BF16),
        scratch_shapes=[pltpu.VMEM((bm, d), BF16)],
        compiler_params=_cparams(("parallel", "arbitrary")),
    )(x, mod, w)


def _s5_operators(lam_re, lam_im, log_step, b_re, b_im, c_re, c_im):
    T = S5_CHUNK
    hi = lax.Precision.HIGHEST
    m_tot, w_cols, v_rows, a_re, a_im = 0.0, [], [], [], []
    for direction in range(2):
        lr = lam_re[direction].astype(F32)
        li = lam_im[direction].astype(F32)
        step = jnp.exp(log_step[direction].astype(F32))[:, None]
        mag = jnp.exp(lr * step)
        ar = mag * jnp.cos(li * step)
        ai = mag * jnp.sin(li * step)
        den = lr * lr + li * li
        fr = ((ar - 1.0) * lr + ai * li) / den
        fi = (ai * lr - (ar - 1.0) * li) / den
        br = b_re[direction].astype(F32)
        bi = b_im[direction].astype(F32)
        bbr = fr[..., None] * br - fi[..., None] * bi
        bbi = fr[..., None] * bi + fi[..., None] * br
        cr = c_re[direction].astype(F32)
        ci = c_im[direction].astype(F32)
        pr, pi = [jnp.ones_like(ar)], [jnp.zeros_like(ar)]
        for _ in range(T):
            pr.append(pr[-1] * ar - pi[-1] * ai)
            pi.append(pr[-2] * ai + pi[-1] * ar)
        pr = jnp.stack(pr)
        pi = jnp.stack(pi)
        abr = pr[:T, :, :, None] * bbr[None] - pi[:T, :, :, None] * bbi[None]
        abi = pr[:T, :, :, None] * bbi[None] + pi[:T, :, :, None] * bbr[None]
        kern = (jnp.einsum('ghp,tgpk->tghk', cr, abr, precision=hi)
                - jnp.einsum('ghp,tgpk->tghk', ci, abi, precision=hi))
        s_idx = jnp.arange(T)[:, None]
        t_idx = jnp.arange(T)[None, :]
        tau = (t_idx - s_idx) if direction == 0 else (s_idx - t_idx)
        ktab = kern[jnp.clip(tau, 0, T - 1)]
        ktab = jnp.where((tau >= 0)[:, :, None, None, None], ktab, 0.0)
        m_tot = m_tot + ktab.transpose(2, 0, 4, 1, 3).reshape(S5_GROUPS, T * S5_GROUP, T * S5_GROUP)
        order = jnp.arange(T - 1, -1, -1) if direction == 0 else jnp.arange(T)
        w_cols.append((abr[order].transpose(1, 0, 3, 2).reshape(S5_GROUPS, T * S5_GROUP, S5_STATE),
                       abi[order].transpose(1, 0, 3, 2).reshape(S5_GROUPS, T * S5_GROUP, S5_STATE)))
        k_idx = (jnp.arange(T) + 1) if direction == 0 else (T - jnp.arange(T))
        qr, qi = pr[k_idx], pi[k_idx]
        v_re = cr[None] * qr[:, :, None, :] - ci[None] * qi[:, :, None, :]
        v_im = -(cr[None] * qi[:, :, None, :] + ci[None] * qr[:, :, None, :])
        v_rows.append((v_re.transpose(1, 3, 0, 2).reshape(S5_GROUPS, S5_STATE, T * S5_GROUP),
                       v_im.transpose(1, 3, 0, 2).reshape(S5_GROUPS, S5_STATE, T * S5_GROUP)))
        a_re.append(pr[T])
        a_im.append(pi[T])
    w_op = jnp.concatenate([w_cols[0][0], w_cols[1][0], w_cols[0][1], w_cols[1][1]], axis=2)
    v_op = jnp.concatenate([v_rows[0][0], v_rows[1][0], v_rows[0][1], v_rows[1][1]], axis=1)
    a_op = jnp.stack([jnp.concatenate(a_re, axis=1), jnp.concatenate(a_im, axis=1)], axis=1)
    return m_tot.astype(BF16), w_op.astype(BF16), v_op.astype(BF16), a_op


def _s5_core_kernel(x_ref, w_ref, m_ref, v_ref, a_ref, y_ref, s_ref, h_ref, *, gb, n_chunks):
    half = S5_STATE
    for g in range(gb):
        s_ref[g] = _dot(x_ref[g], w_ref[g])
    is_fwd = lax.broadcasted_iota(jnp.int32, (SUBLANES, LANES), 1) < half
    a_re = [jnp.broadcast_to(a_ref[g, 0:1, :], (SUBLANES, LANES)) for g in range(gb)]
    a_im = [jnp.broadcast_to(a_ref[g, 1:2, :], (SUBLANES, LANES)) for g in range(gb)]

    def body(i, carry):
        rf = pl.multiple_of(i * SUBLANES, SUBLANES)
        rb = pl.multiple_of((n_chunks - 1 - i) * SUBLANES, SUBLANES)
        new = []
        for g in range(gb):
            h0, h1 = carry[g]
            h_ref[g, pl.ds(rf, SUBLANES), 0:half] = h0[:, :half]
            h_ref[g, pl.ds(rf, SUBLANES), LANES:LANES + half] = h1[:, :half]
            h_ref[g, pl.ds(rb, SUBLANES), half:LANES] = h0[:, half:]
            h_ref[g, pl.ds(rb, SUBLANES), LANES + half:2 * LANES] = h1[:, half:]
            sf = s_ref[g, pl.ds(rf, SUBLANES), :]
            sb = s_ref[g, pl.ds(rb, SUBLANES), :]
            s0 = jnp.where(is_fwd, sf[:, :LANES], sb[:, :LANES])
            s1 = jnp.where(is_fwd, sf[:, LANES:], sb[:, LANES:])
            new.append((a_re[g] * h0 - a_im[g] * h1 + s0, a_re[g] * h1 + a_im[g] * h0 + s1))
        return tuple(new)

    zero = jnp.zeros((SUBLANES, LANES), F32)
    lax.fori_loop(0, n_chunks, body, tuple((zero, zero) for _ in range(gb)))
    for g in range(gb):
        y = _dot(x_ref[g], m_ref[g]) + _dot(h_ref[g].astype(BF16), v_ref[g])
        y_ref[g] = y.astype(y_ref.dtype)


def _s5_core(xc, w_op, m_op, v_op, a_op, n_chunks, gb):
    groups, rows, width = xc.shape
    op_spec = pl.BlockSpec((gb, width, width), lambda i: (i, 0, 0))
    return pl.pallas_call(
        functools.partial(_s5_core_kernel, gb=gb, n_chunks=n_chunks),
        grid=(groups // gb,),
        in_specs=[pl.BlockSpec((gb, rows, width), lambda i: (i, 0, 0)),
                  op_spec, op_spec, op_spec,
                  pl.BlockSpec((gb, 2, LANES), lambda i: (i, 0, 0))],
        out_specs=pl.BlockSpec((gb, rows, width), lambda i: (i, 0, 0)),
        out_shape=jax.ShapeDtypeStruct((groups, rows, width), BF16),
        scratch_shapes=[pltpu.VMEM((gb, rows, width), F32), pltpu.VMEM((gb, rows, width), F32)],
        compiler_params=_cparams(("parallel",)),
    )(xc, w_op, m_op, v_op, a_op)


def _to_chunks(u, batch, length):
    c = length // S5_CHUNK
    x = u.reshape(batch, c, S5_CHUNK, S5_GROUPS, S5_GROUP).transpose(3, 1, 0, 2, 4)
    x = jnp.pad(x, ((0, 0), (0, 0), (0, SUBLANES - batch), (0, 0), (0, 0)))
    return x.reshape(S5_GROUPS, c * SUBLANES, S5_CHUNK * S5_GROUP)


def _from_chunks(y, batch, length):
    c = length // S5_CHUNK
    y = y.reshape(S5_GROUPS, c, SUBLANES, S5_CHUNK, S5_GROUP)[:, :, :batch]
    return y.transpose(2, 1, 3, 0, 4).reshape(batch * length, S5_WIDTH)


def _s5_post_kernel(y_ref, u_ref, d_ref, wg_ref, bg_ref, z_ref):
    y = d_ref[...] * u_ref[...].astype(F32) + y_ref[...].astype(F32)
    act = 0.5 * y * (1.0 + jnp.tanh(math.sqrt(2.0 / math.pi) * (y + 0.044715 * (y * y * y))))
    gl = _dot(act.astype(BF16), wg_ref[...]) + bg_ref[...]
    z_ref[...] = (act * _sigmoid(gl)).astype(z_ref.dtype)


def _s5_post(y, proj, d, wg, bg, bm):
    n, w = y.shape
    return pl.pallas_call(
        _s5_post_kernel,
        grid=(n // bm,),
        in_specs=[pl.BlockSpec((bm, w), lambda i: (i, 0)),
                  pl.BlockSpec((bm, w), lambda i: (i, COL_U // S5_WIDTH)),
                  pl.BlockSpec((1, w), lambda i: (0, 0)),
                  pl.BlockSpec((w, w), lambda i: (0, 0)),
                  pl.BlockSpec((1, w), lambda i: (0, 0))],
        out_specs=pl.BlockSpec((bm, w), lambda i: (i, 0)),
        out_shape=jax.ShapeDtypeStruct((n, w), BF16),
        compiler_params=_cparams(("parallel",)),
    )(y, proj, d.reshape(1, w), wg, bg.reshape(1, w))


def _rms_norm(x, g):
    return x * lax.rsqrt(jnp.mean(x * x, axis=-1, keepdims=True) + RMS_EPS) * g


def _rope_block(r, cos, sin):
    return r * cos + pltpu.roll(r, LANES // 2, 1) * sin


def _qproj_kernel(cq_ref, g_ref, w_ref, cos_ref, sin_ref, q_ref, *, heads, scale):
    xn = _rms_norm(cq_ref[...].astype(F32), g_ref[...]).astype(BF16)
    acc = _dot(xn, w_ref[...])
    cos, sin = cos_ref[...], sin_ref[...]
    for h in range(heads):
        lo = h * HEAD_PAD
        q_ref[:, lo:lo + LANES] = (acc[:, lo:lo + LANES] * scale).astype(q_ref.dtype)
        rope = _rope_block(acc[:, lo + LANES:lo + HEAD_PAD], cos, sin)
        q_ref[:, lo + LANES:lo + HEAD_PAD] = (rope * scale).astype(q_ref.dtype)


def _qproj(proj, g, w, cos, sin, bm, pos_block, heads_per_step=4):
    n = proj.shape[0]
    nq = w.shape[1]
    bn = heads_per_step * HEAD_PAD
    scale = (QK_NOPE + QK_ROPE) ** -0.5
    return pl.pallas_call(
        functools.partial(_qproj_kernel, heads=heads_per_step, scale=scale),
        grid=(n // bm, nq // bn),
        in_specs=[pl.BlockSpec((bm, Q_LORA), lambda i, j: (i, COL_CQ // Q_LORA)),
                  pl.BlockSpec((1, Q_LORA), lambda i, j: (0, 0)),
                  pl.BlockSpec((Q_LORA, bn), lambda i, j: (0, j)),
                  pl.BlockSpec((bm, LANES), lambda i, j: (pos_block(i), 0)),
                  pl.BlockSpec((bm, LANES), lambda i, j: (pos_block(i), 0))],
        out_specs=pl.BlockSpec((bm, bn), lambda i, j: (i, j)),
        out_shape=jax.ShapeDtypeStruct((n, nq), BF16),
        compiler_params=_cparams(("parallel", "parallel")),
    )(proj, g.reshape(1, Q_LORA), w, cos, sin)


def _kvproj_kernel(ckv_ref, kr_ref, g_ref, wk_ref, wv_ref, cos_ref, sin_ref, k_ref, v_ref):
    xn = _rms_norm(ckv_ref[...].astype(F32), g_ref[...]).astype(BF16)
    kn = _dot(xn, wk_ref[...])
    v_ref[...] = _dot(xn, wv_ref[...]).astype(v_ref.dtype)
    kro = _rope_block(kr_ref[...].astype(F32), cos_ref[...], sin_ref[...]).astype(k_ref.dtype)
    for h in range(MLA_HEADS):
        lo = h * HEAD_PAD
        k_ref[:, lo:lo + LANES] = kn[:, h * QK_NOPE:(h + 1) * QK_NOPE].astype(k_ref.dtype)
        k_ref[:, lo + LANES:lo + HEAD_PAD] = kro


def _kvproj(proj, g, wk, wv, cos, sin, bm, pos_block):
    n = proj.shape[0]
    return pl.pallas_call(
        _kvproj_kernel,
        grid=(n // bm,),
        in_specs=[pl.BlockSpec((bm, KV_LORA), lambda i: (i, COL_CKV // KV_LORA)),
                  pl.BlockSpec((bm, LANES), lambda i: (i, COL_KR // LANES)),
                  pl.BlockSpec((1, KV_LORA), lambda i: (0, 0)),
                  pl.BlockSpec(wk.shape, lambda i: (0, 0)),
                  pl.BlockSpec(wv.shape, lambda i: (0, 0)),
                  pl.BlockSpec((bm, LANES), lambda i: (pos_block(i), 0)),
                  pl.BlockSpec((bm, LANES), lambda i: (pos_block(i), 0))],
        out_specs=[pl.BlockSpec((bm, MLA_HEADS * HEAD_PAD), lambda i: (i, 0)),
                   pl.BlockSpec((bm, MLA_HEADS * V_HEAD), lambda i: (i, 0))],
        out_shape=[jax.ShapeDtypeStruct((n, MLA_HEADS * HEAD_PAD), BF16),
                   jax.ShapeDtypeStruct((n, MLA_HEADS * V_HEAD), BF16)],
        compiler_params=_cparams(("parallel",)),
    )(proj, proj, g.reshape(1, KV_LORA), wk, wv, cos, sin)


def _attn_kernel(q_ref, k_ref, v_ref, o_ref):
    s = lax.dot_general(q_ref[...], k_ref[...], (((1,), (1,)), ((), ())), preferred_element_type=F32)
    m = jnp.max(s, axis=-1, keepdims=True)
    p = jnp.exp(s - m)
    l = jnp.sum(p, axis=-1, keepdims=True)
    o = _dot(p.astype(BF16), v_ref[...])
    o_ref[...] = (o / l).astype(o_ref.dtype)


def _attention(q, k, v, base, batch, length, tq):
    nq = length // tq
    row0 = base // tq
    seq0 = base // length
    return pl.pallas_call(
        _attn_kernel,
        grid=(batch, MLA_HEADS, nq),
        in_specs=[pl.BlockSpec((tq, HEAD_PAD), lambda b, h, i: (row0 + b * nq + i, h)),
                  pl.BlockSpec((length, HEAD_PAD), lambda b, h, i: (seq0 + b, h)),
                  pl.BlockSpec((length, V_HEAD), lambda b, h, i: (seq0 + b, h))],
        out_specs=pl.BlockSpec((tq, V_HEAD), lambda b, h, i: (b * nq + i, h)),
        out_shape=jax.ShapeDtypeStruct((batch * length, MLA_HEADS * V_HEAD), BF16),
        compiler_params=_cparams(("parallel", "parallel", "arbitrary")),
    )(q, k, v)


def _merge_kernel(z_ref, o_ref, gs_ref, gm_ref, wsp_ref, wmp_ref, out_ref):
    ys = _dot(z_ref[...], wsp_ref[...])
    ym = _dot(o_ref[...], wmp_ref[...])
    out = _sigmoid(gs_ref[...].astype(F32)) * ys + _sigmoid(gm_ref[...].astype(F32)) * ym
    out_ref[...] = out.astype(out_ref.dtype)


def _merge(z, o, proj, wsp, wmp, bm):
    n = z.shape[0]
    d = D_MODEL
    return pl.pallas_call(
        _merge_kernel,
        grid=(n // bm,),
        in_specs=[pl.BlockSpec((bm, S5_WIDTH), lambda i: (i, 0)),
                  pl.BlockSpec((bm, d), lambda i: (i, 0)),
                  pl.BlockSpec((bm, d), lambda i: (i, COL_GS // d)),
                  pl.BlockSpec((bm, d), lambda i: (i, COL_GM // d)),
                  pl.BlockSpec(wsp.shape, lambda i: (0, 0)),
                  pl.BlockSpec(wmp.shape, lambda i: (0, 0))],
        out_specs=pl.BlockSpec((bm, d), lambda i: (i, 0)),
        out_shape=jax.ShapeDtypeStruct((n, d), BF16),
        compiler_params=_cparams(("parallel",)),
    )(z, o, proj, proj, wsp, wmp)


def _out_kernel(mg_ref, x_ref, mod_ref, wo_ref, lng_ref, lnb_ref, wr_ref, xo_ref, h2_ref, aff_ref):
    m = mod_ref[0]
    out = _dot(mg_ref[...], wo_ref[...])
    xn = _layer_norm(DEEPNORM_ALPHA * x_ref[...] + m[2:3] * out, lng_ref[...], lnb_ref[...])
    xo_ref[...] = xn
    h2 = (xn * (1.0 + m[4:5]) + m[3:4]).astype(BF16)
    h2_ref[...] = h2
    logits = _dot(h2, wr_ref[...])
    lane = lax.broadcasted_iota(jnp.int32, logits.shape, 1)
    logits = jnp.where(lane < N_EXPERTS, logits, -jnp.inf)
    e = jnp.exp(logits - jnp.max(logits, axis=-1, keepdims=True))
    aff_ref[...] = e / jnp.sum(e, axis=-1, keepdims=True)


def _out_proj(merged, x, mod, wo, lng, lnb, wr, bm, seg):
    n, d = x.shape
    return pl.pallas_call(
        _out_kernel,
        grid=(n // bm,),
        in_specs=[pl.BlockSpec((bm, d), lambda i: (i, 0)),
                  pl.BlockSpec((bm, d), lambda i: (i, 0)),
                  pl.BlockSpec((1, 6, d), lambda i: ((i * bm) // seg, 0, 0)),
                  pl.BlockSpec((d, d), lambda i: (0, 0)),
                  pl.BlockSpec((1, d), lambda i: (0, 0)),
                  pl.BlockSpec((1, d), lambda i: (0, 0)),
                  pl.BlockSpec((d, LANES), lambda i: (0, 0))],
        out_specs=[pl.BlockSpec((bm, d), lambda i: (i, 0)),
                   pl.BlockSpec((bm, d), lambda i: (i, 0)),
                   pl.BlockSpec((bm, LANES), lambda i: (i, 0))],
        out_shape=[jax.ShapeDtypeStruct((n, d), F32),
                   jax.ShapeDtypeStruct((n, d), BF16),
                   jax.ShapeDtypeStruct((n, LANES), F32)],
        compiler_params=_cparams(("parallel",)),
    )(merged, x, mod, wo, lng.reshape(1, d), lnb.reshape(1, d), wr)


def _select_kernel(aff_ref, gate_ref, cnt_ref, off_ref, *, n, cap, bm):
    aff = aff_ref[...]
    bits = pltpu.bitcast(aff, jnp.int32)
    idx = lax.broadcasted_iota(jnp.int32, aff.shape, 0)
    valid = lax.broadcasted_iota(jnp.int32, (1, LANES), 1) < N_EXPERTS

    def count(mask):
        return jnp.sum(mask.astype(jnp.int32), axis=0, keepdims=True)

    def thr_step(k, thr):
        cand = thr | jnp.left_shift(jnp.int32(1), 30 - k)
        return jnp.where(count(bits >= cand) >= cap, cand, thr)

    thr = lax.fori_loop(0, 31, thr_step, jnp.zeros((1, LANES), jnp.int32))
    above = bits > thr
    tied = bits == thr
    need = cap - count(above)
    idx_bits = max(1, (n - 1).bit_length())

    def idx_step(k, bound):
        cand = bound | jnp.left_shift(jnp.int32(1), idx_bits - 1 - k)
        return jnp.where(count(tied & (idx < cand)) < need, cand, bound)

    bound = lax.fori_loop(0, idx_bits, idx_step, jnp.zeros((1, LANES), jnp.int32))
    sel = (above | (tied & (idx <= bound))) & valid
    gate_ref[...] = jnp.where(sel, aff, -1.0)
    nt = n // bm
    cnt = jnp.sum(sel.astype(F32).reshape(nt, bm, LANES), axis=1)
    r = lax.broadcasted_iota(jnp.int32, (nt, nt), 0)
    c = lax.broadcasted_iota(jnp.int32, (nt, nt), 1)
    off = _dot((c < r).astype(BF16), cnt.astype(BF16))
    cnt_ref[0] = cnt.astype(jnp.int32)
    off_ref[0] = off.astype(jnp.int32)


def _select(aff, n_sets, n, cap, bm):
    nt = n // bm
    return pl.pallas_call(
        functools.partial(_select_kernel, n=n, cap=cap, bm=bm),
        grid=(n_sets,),
        in_specs=[pl.BlockSpec((n, LANES), lambda g: (g, 0))],
        out_specs=[pl.BlockSpec((n, LANES), lambda g: (g, 0)),
                   pl.BlockSpec((1, nt, LANES), lambda g: (g, 0, 0)),
                   pl.BlockSpec((1, nt, LANES), lambda g: (g, 0, 0))],
        out_shape=[jax.ShapeDtypeStruct((n_sets * n, LANES), F32),
                   jax.ShapeDtypeStruct((n_sets, nt, LANES), jnp.int32),
                   jax.ShapeDtypeStruct((n_sets, nt, LANES), jnp.int32)],
        compiler_params=_cparams(("parallel",)),
    )(aff)


def _slot_onehot(gate, pos, offs, win_lo, dma_lo, weighted):
    bm = gate.shape[0]
    j = lax.broadcasted_iota(jnp.int32, (bm, MOE_SLAB), 1).astype(F32)
    cols = []
    for e in range(N_EXPERTS):
        ge = gate[:, e:e + 1]
        slot = pos[:, e:e + 1] + offs[e].astype(F32)
        lo = win_lo[e].astype(F32)
        hit = (ge >= 0.0) & (slot >= lo) & (slot < lo + MOE_SLAB) & ((slot - dma_lo[e].astype(F32)) == j)
        cols.append(jnp.where(hit, ge if weighted else 1.0, 0.0).astype(BF16))
    return jnp.concatenate(cols, axis=1)


def _slot_positions(gate):
    bm = gate.shape[0]
    r = lax.broadcasted_iota(jnp.int32, (bm, bm), 0)
    c = lax.broadcasted_iota(jnp.int32, (bm, bm), 1)
    return _dot((c < r).astype(BF16), (gate >= 0.0).astype(BF16))


def _align_down(v):
    return (v // SLOT_ALIGN) * SLOT_ALIGN


def _compact_kernel(off_ref, npass_ref, coff_ref, nsub_ref, h_ref, gate_ref, xe_hbm,
                    slab_ref, carry_ref, sem, *, nt, cap):
    g = pl.program_id(0)
    i = pl.program_id(1)

    @pl.when(i == 0)
    def _():
        carry_ref[...] = jnp.zeros_like(carry_ref)
        slab_ref[...] = jnp.zeros_like(slab_ref)
        tails = [pltpu.make_async_copy(slab_ref.at[pl.ds(e * MOE_SLAB, MOE_SLAB)],
                                       xe_hbm.at[g, e, pl.ds(cap, MOE_SLAB)], sem.at[e]) for e in range(N_EXPERTS)]
        for cp in tails:
            cp.start()
        for cp in tails:
            cp.wait()

    gate = gate_ref[...]
    pos = _slot_positions(gate)
    base = (g * nt + i) * N_EXPERTS
    offs = [off_ref[base + e] for e in range(N_EXPERTS)]
    npass = [npass_ref[base + e] for e in range(N_EXPERTS)]

    def one_pass(s, carry):
        lo = [_align_down(offs[e]) + s * MOE_SLAB for e in range(N_EXPERTS)]
        sel_t = _slot_onehot(gate, pos, offs, lo, lo, weighted=False)
        slab = lax.dot_general(sel_t, h_ref[...], (((0,), (0,)), ((), ())), preferred_element_type=F32)
        slab_ref[...] = slab.astype(slab_ref.dtype)

        @pl.when(s == 0)
        def _():
            for e in range(N_EXPERTS):
                rows = pl.ds(e * MOE_SLAB, SLOT_ALIGN)
                merged = slab_ref[rows, :].astype(F32) + carry_ref[pl.ds(e * SLOT_ALIGN, SLOT_ALIGN), :].astype(F32)
                slab_ref[rows, :] = merged.astype(slab_ref.dtype)

        for e in range(N_EXPERTS):
            @pl.when(s == npass[e] - 1)
            def _(e=e):
                src = pl.multiple_of(e * MOE_SLAB + coff_ref[base + e], SLOT_ALIGN)
                carry_ref[pl.ds(e * SLOT_ALIGN, SLOT_ALIGN), :] = slab_ref[pl.ds(src, SLOT_ALIGN), :]

        def copy(e):
            return pltpu.make_async_copy(
                slab_ref.at[pl.ds(e * MOE_SLAB, MOE_SLAB)],
                xe_hbm.at[g, e, pl.ds(pl.multiple_of(lo[e], SLOT_ALIGN), MOE_SLAB)], sem.at[e])

        for e in range(N_EXPERTS):
            @pl.when(s < npass[e])
            def _(e=e):
                copy(e).start()
        for e in range(N_EXPERTS):
            @pl.when(s < npass[e])
            def _(e=e):
                copy(e).wait()
        return carry

    lax.fori_loop(0, nsub_ref[g * nt + i], one_pass, 0)


def _compact(off, npass, coff, nsub, h2, gate, n_sets, n, cap, bm):
    nt = n // bm
    d = h2.shape[1]
    grid_spec = pltpu.PrefetchScalarGridSpec(
        num_scalar_prefetch=4,
        grid=(n_sets, nt),
        in_specs=[pl.BlockSpec((bm, d), lambda g, i, *_: (g * nt + i, 0)),
                  pl.BlockSpec((bm, LANES), lambda g, i, *_: (g * nt + i, 0))],
        out_specs=pl.BlockSpec(memory_space=pl.ANY),
        scratch_shapes=[pltpu.VMEM((N_EXPERTS * MOE_SLAB, d), BF16),
                        pltpu.VMEM((N_EXPERTS * SLOT_ALIGN, d), BF16),
                        pltpu.SemaphoreType.DMA((N_EXPERTS,))],
    )
    return pl.pallas_call(
        functools.partial(_compact_kernel, nt=nt, cap=cap),
        grid_spec=grid_spec,
        out_shape=jax.ShapeDtypeStruct((n_sets, N_EXPERTS, cap + MOE_SLAB, d), BF16),
        compiler_params=_cparams(("arbitrary", "arbitrary")),
    )(off, npass, coff, nsub, h2, gate)


def _ffn_kernel(xe_ref, w1_ref, w3_ref, w2_ref, ye_ref):
    x = xe_ref[0, 0]
    a = _dot(x, w1_ref[0])
    b = _dot(x, w3_ref[0])
    hid = (a * _sigmoid(a) * b).astype(BF16)
    ye_ref[0, 0] = _dot(hid, w2_ref[0]).astype(ye_ref.dtype)


def _ffn(xe, w1, w3, w2, n_sets, cap, tm):
    d = xe.shape[-1]
    f = w1.shape[-1]
    return pl.pallas_call(
        _ffn_kernel,
        grid=(N_EXPERTS, n_sets, cap // tm),
        in_specs=[pl.BlockSpec((1, 1, tm, d), lambda e, g, t: (g, e, t, 0)),
                  pl.BlockSpec((1, d, f), lambda e, g, t: (e, 0, 0)),
                  pl.BlockSpec((1, d, f), lambda e, g, t: (e, 0, 0)),
                  pl.BlockSpec((1, f, d), lambda e, g, t: (e, 0, 0))],
        out_specs=pl.BlockSpec((1, 1, tm, d), lambda e, g, t: (g, e, t, 0)),
        out_shape=jax.ShapeDtypeStruct((n_sets, N_EXPERTS, cap, d), BF16),
        compiler_params=_cparams(("parallel", "parallel", "parallel")),
    )(xe, w1, w3, w2)


def _combine_kernel(off_ref, nsub_ref, gate_ref, x_ref, mod_ref, lng_ref, lnb_ref, ye_hbm,
                    xo_ref, slab_ref, sem, *, nt, cap):
    g = pl.program_id(0)
    i = pl.program_id(1)
    gate = gate_ref[...]
    pos = _slot_positions(gate)
    base = (g * nt + i) * N_EXPERTS

    offs = [off_ref[base + e] for e in range(N_EXPERTS)]

    def one_pass(s, acc):
        lo = [_align_down(offs[e]) + s * MOE_SLAB for e in range(N_EXPERTS)]
        start = [jnp.minimum(lo[e], cap - MOE_SLAB) for e in range(N_EXPERTS)]
        copies = [pltpu.make_async_copy(
            ye_hbm.at[g, e, pl.ds(pl.multiple_of(start[e], SLOT_ALIGN), MOE_SLAB)],
            slab_ref.at[pl.ds(e * MOE_SLAB, MOE_SLAB)], sem.at[e]) for e in range(N_EXPERTS)]
        for cp in copies:
            cp.start()
        for cp in copies:
            cp.wait()
        spread = _slot_onehot(gate, pos, offs, lo, start, weighted=True)
        return acc + _dot(spread, slab_ref[...])

    moe = lax.fori_loop(0, nsub_ref[g * nt + i], one_pass, jnp.zeros(x_ref.shape, F32))
    m = mod_ref[0]
    xo_ref[...] = _layer_norm(DEEPNORM_ALPHA * x_ref[...] + m[5:6] * moe, lng_ref[...], lnb_ref[...])


def _combine(off, nsub, gate, x, mod, lng, lnb, ye, n_sets, n, cap, bm, seg):
    nt = n // bm
    d = x.shape[1]
    grid_spec = pltpu.PrefetchScalarGridSpec(
        num_scalar_prefetch=2,
        grid=(n_sets, nt),
        in_specs=[pl.BlockSpec((bm, LANES), lambda g, i, *_: (g * nt + i, 0)),
                  pl.BlockSpec((bm, d), lambda g, i, *_: (g * nt + i, 0)),
                  pl.BlockSpec((1, 6, d), lambda g, i, *_: (((g * nt + i) * bm) // seg, 0, 0)),
                  pl.BlockSpec((1, d), lambda g, i, *_: (0, 0)),
                  pl.BlockSpec((1, d), lambda g, i, *_: (0, 0)),
                  pl.BlockSpec(memory_space=pl.ANY)],
        out_specs=pl.BlockSpec((bm, d), lambda g, i, *_: (g * nt + i, 0)),
        scratch_shapes=[pltpu.VMEM((N_EXPERTS * MOE_SLAB, d), BF16),
                        pltpu.SemaphoreType.DMA((N_EXPERTS,))],
    )
    return pl.pallas_call(
        functools.partial(_combine_kernel, nt=nt, cap=cap),
        grid_spec=grid_spec,
        out_shape=jax.ShapeDtypeStruct(x.shape, F32),
        compiler_params=_cparams(("arbitrary", "arbitrary")),
    )(off, nsub, gate, x, mod, lng.reshape(1, d), lnb.reshape(1, d), ye)


def _rope_lanes(w):
    z = jnp.zeros(w.shape[:-1] + (ROPE_HALF,), w.dtype)
    return jnp.concatenate([w[..., :ROPE_HALF], z, w[..., ROPE_HALF:], z], axis=-1)


def _proj_weight(w_in):
    c0 = S5_WIDTH + Q_LORA + KV_LORA
    kr = w_in[:, c0:c0 + QK_ROPE]
    gates = w_in[:, c0 + QK_ROPE:]
    return jnp.concatenate([w_in[:, :c0], gates, _rope_lanes(kr)], axis=1).astype(BF16)


def _q_weight(w_uq):
    w = w_uq.reshape(Q_LORA, MLA_HEADS, QK_NOPE + QK_ROPE)
    w = jnp.concatenate([w[..., :QK_NOPE], _rope_lanes(w[..., QK_NOPE:])], axis=-1)
    return w.reshape(Q_LORA, MLA_HEADS * HEAD_PAD).astype(BF16)


def _kv_weights(w_ukv):
    w = w_ukv.reshape(KV_LORA, MLA_HEADS, QK_NOPE + V_HEAD)
    wk = w[..., :QK_NOPE].reshape(KV_LORA, MLA_HEADS * QK_NOPE)
    wv = w[..., QK_NOPE:].reshape(KV_LORA, MLA_HEADS * V_HEAD)
    return wk.astype(BF16), wv.astype(BF16)


def _rope_tables(length):
    pos = jnp.arange(length, dtype=F32)
    inv = 1.0 / (ROPE_BASE ** (jnp.arange(0, QK_ROPE, 2, dtype=F32) / QK_ROPE))
    ang = pos[:, None] * inv[None, :]
    cos, sin = jnp.cos(ang), jnp.sin(ang)
    z = jnp.zeros_like(cos)
    return (jnp.concatenate([cos, z, cos, z], axis=1), jnp.concatenate([-sin, z, sin, z], axis=1))


def _trunk(x_groups, c_groups, p):
    (b1, l1, d), (b2, l2, _) = x_groups[0].shape, x_groups[1].shape
    n1, n2 = b1 * l1, b2 * l2
    n = n1 + n2
    seg = l1
    assert l2 % l1 == 0 and n1 == n2 and n1 % l2 == 0
    cap = EC_CAPACITY * n1 // N_EXPERTS
    bm = min(512, seg)
    bm_in = min(1024, seg)
    bm_moe = min(256, seg)
    tq = min(256, l1)
    tm = min(256, cap)
    gb = 2
    assert cap >= MOE_SLAB and cap % tm == 0

    x = jnp.concatenate([x_groups[0].reshape(n1, d), x_groups[1].reshape(n2, d)], axis=0)
    x = _ln_in(x, p['ln_in_g'], p['ln_in_b'], bm)

    c = jnp.concatenate(list(c_groups), axis=0)
    c_rows = -(-c.shape[0] // SUBLANES) * SUBLANES
    mod_all = _ada(jnp.pad(c, ((0, c_rows - c.shape[0]), (0, 0))), p['w_ada'], p['b_ada'])
    seg_rows = jnp.array(list(range(b1)) + [b1 + b for b in range(b2) for _ in range(l2 // seg)])
    mod_all = mod_all.reshape(DEPTH, c_rows, 6, d)[:, seg_rows]

    cos, sin = _rope_tables(max(l1, l2))

    def pos_block(bm_):
        def f(i):
            return jnp.where(i < n1 // bm_, i % (l1 // bm_), (i - n1 // bm_) % (l2 // bm_))
        return f

    for l in range(DEPTH):
        mod = mod_all[l]
        proj = _inproj(x, mod, _proj_weight(p['w_in'][l]), bm_in, PROJ_W // 7, seg)

        m_op, w_op, v_op, a_op = _s5_operators(p['s5_lam_re'][l], p['s5_lam_im'][l], p['s5_log_step'][l],
                                               p['s5_b_re'][l], p['s5_b_im'][l], p['s5_c_re'][l], p['s5_c_im'][l])
        u = proj[:, :S5_WIDTH]
        ys = []
        for base, bt, ln in ((0, b1, l1), (n1, b2, l2)):
            xc = _to_chunks(u[base:base + bt * ln], bt, ln)
            yc = _s5_core(xc, w_op, m_op, v_op, a_op, ln // S5_CHUNK, gb)
            ys.append(_from_chunks(yc, bt, ln))
        z = _s5_post(jnp.concatenate(ys, axis=0), proj, p['s5_d'][l], p['s5_w_glu'][l].astype(BF16),
                     p['s5_b_glu'][l], bm)

        q = _qproj(proj, p['mla_q_norm'][l], _q_weight(p['mla_w_uq'][l]), cos, sin, bm, pos_block(bm))
        wk, wv = _kv_weights(p['mla_w_ukv'][l])
        k, v = _kvproj(proj, p['mla_kv_norm'][l], wk, wv, cos, sin, bm, pos_block(bm))
        o = jnp.concatenate([_attention(q, k, v, 0, b1, l1, tq), _attention(q, k, v, n1, b2, l2, tq)], axis=0)

        merged = _merge(z, o, proj, p['w_s5_proj'][l].astype(BF16), p['w_mla_proj'][l].astype(BF16), bm)
        wr = jnp.pad(p['w_router'][l], ((0, 0), (0, LANES - N_EXPERTS))).astype(BF16)
        x, h2, aff = _out_proj(merged, x, mod, p['w_o'][l].astype(BF16), p['ln1_g'][l], p['ln1_b'][l],
                               wr, bm, seg)

        gate, cnt, off = _select(aff, 2, n1, cap, bm_moe)
        cnt, off = cnt[..., :N_EXPERTS], off[..., :N_EXPERTS]
        filled = off % SLOT_ALIGN + cnt
        npass = filled // MOE_SLAB + 1
        coff = (filled // SLOT_ALIGN) * SLOT_ALIGN - (npass - 1) * MOE_SLAB
        nsub = jnp.max(npass, axis=-1).reshape(-1)
        off, npass, coff = off.reshape(-1), npass.reshape(-1), coff.reshape(-1)
        xe = _compact(off, npass, coff, nsub, h2, gate, 2, n1, cap, bm_moe)
        ye = _ffn(xe, p['w_exp1'][l].astype(BF16), p['w_exp3'][l].astype(BF16),
                  p['w_exp2'][l].astype(BF16), 2, cap, tm)
        x = _combine(off, nsub, gate, x, mod, p['ln2_g'][l], p['ln2_b'][l], ye, 2, n1, cap, bm_moe, seg)

    return x[:n1].reshape(b1, l1, d), x[n1:].reshape(b2, l2, d)


def kernel(x_prompt, x_sample, c_prompt, c_sample, ln_in_g, ln_in_b, w_ada, b_ada, w_in, s5_lam_re, s5_lam_im, s5_log_step, s5_b_re, s5_b_im, s5_c_re, s5_c_im, s5_d, s5_w_glu, s5_b_glu, w_s5_proj, mla_q_norm, mla_w_uq, mla_kv_norm, mla_w_ukv, w_mla_proj, w_o, ln1_g, ln1_b, w_router, w_exp1, w_exp3, w_exp2, ln2_g, ln2_b):
    params = dict(ln_in_g=ln_in_g, ln_in_b=ln_in_b, w_ada=w_ada, b_ada=b_ada, w_in=w_in,
                  s5_lam_re=s5_lam_re, s5_lam_im=s5_lam_im, s5_log_step=s5_log_step,
                  s5_b_re=s5_b_re, s5_b_im=s5_b_im, s5_c_re=s5_c_re, s5_c_im=s5_c_im, s5_d=s5_d,
                  s5_w_glu=s5_w_glu, s5_b_glu=s5_b_glu, w_s5_proj=w_s5_proj,
                  mla_q_norm=mla_q_norm, mla_w_uq=mla_w_uq, mla_kv_norm=mla_kv_norm,
                  mla_w_ukv=mla_w_ukv, w_mla_proj=w_mla_proj, w_o=w_o, ln1_g=ln1_g, ln1_b=ln1_b,
                  w_router=w_router, w_exp1=w_exp1, w_exp3=w_exp3, w_exp2=w_exp2,
                  ln2_g=ln2_g, ln2_b=ln2_b)
    return _trunk((x_prompt, x_sample), (c_prompt, c_sample), params)
```

```python
import functools
import math

import jax
import jax.numpy as jnp
from jax import lax
from jax.experimental import pallas as pl
from jax.experimental.pallas import tpu as pltpu

F32 = jnp.float32
BF16 = jnp.bfloat16

D_MODEL = 2048
DEPTH = 4
S5_WIDTH = D_MODEL // 2
S5_GROUP = 16
S5_GROUPS = S5_WIDTH // S5_GROUP
S5_STATE = 64
MLA_HEADS = 16
QK_NOPE = 128
QK_ROPE = 64
V_HEAD = 128
Q_LORA = 512
KV_LORA = 512
ROPE_BASE = 10000.0
N_EXPERTS = 16
EC_CAPACITY = 2
D_EXPERT = 1408
DEEPNORM_ALPHA = (2 * DEPTH) ** 0.25
LN_EPS = 1e-5
RMS_EPS = 1e-6

LANES = 128
SUBLANES = 8
MXU_DIM = 256
S5_CHUNK = MXU_DIM // S5_GROUP
HEAD_PAD = 2 * LANES
ROPE_HALF = QK_ROPE // 2
COL_U, COL_CQ, COL_CKV = 0, S5_WIDTH, S5_WIDTH + Q_LORA
COL_GS = S5_WIDTH + Q_LORA + KV_LORA
COL_GM = COL_GS + D_MODEL
COL_KR = COL_GM + D_MODEL
PROJ_W = COL_KR + LANES
MOE_SLAB = 64
SLOT_ALIGN = 16
VMEM_LIMIT = 56 * 1024 * 1024


def _cparams(sem, vmem=VMEM_LIMIT):
    return pltpu.CompilerParams(dimension_semantics=sem, vmem_limit_bytes=vmem)


def _dot(a, b):
    return jnp.dot(a, b, preferred_element_type=F32)


def _layer_norm(y, g, b):
    mu = jnp.mean(y, axis=-1, keepdims=True)
    yc = y - mu
    var = jnp.mean(yc * yc, axis=-1, keepdims=True)
    return yc * lax.rsqrt(var + LN_EPS) * g + b


def _sigmoid(x):
    return 1.0 / (1.0 + jnp.exp(-x))


def _ln_in_kernel(x_ref, g_ref, b_ref, o_ref):
    o_ref[...] = _layer_norm(x_ref[...], g_ref[...], b_ref[...])


def _ln_in(x, g, b, bm):
    n, d = x.shape
    return pl.pallas_call(
        _ln_in_kernel,
        grid=(n // bm,),
        in_specs=[pl.BlockSpec((bm, d), lambda i: (i, 0)),
                  pl.BlockSpec((1, d), lambda i: (0, 0)),
                  pl.BlockSpec((1, d), lambda i: (0, 0))],
        out_specs=pl.BlockSpec((bm, d), lambda i: (i, 0)),
        out_shape=jax.ShapeDtypeStruct((n, d), F32),
        compiler_params=_cparams(("parallel",)),
    )(x, g.reshape(1, d), b.reshape(1, d))


def _ada_kernel(c_ref, w_ref, b_ref, o_ref):
    c = c_ref[...]
    cond = (c * _sigmoid(c)).astype(BF16)
    o_ref[0] = _dot(cond, w_ref[0].astype(BF16)) + b_ref[0]


def _ada(c_pad, w_ada, b_ada, bn=1024):
    depth, d, n6 = w_ada.shape
    rows = c_pad.shape[0]
    return pl.pallas_call(
        _ada_kernel,
        grid=(depth, n6 // bn),
        in_specs=[pl.BlockSpec((rows, d), lambda l, j: (0, 0)),
                  pl.BlockSpec((1, d, bn), lambda l, j: (l, 0, j)),
                  pl.BlockSpec((1, 1, bn), lambda l, j: (l, 0, j))],
        out_specs=pl.BlockSpec((1, rows, bn), lambda l, j: (l, 0, j)),
        out_shape=jax.ShapeDtypeStruct((depth, rows, n6), F32),
        compiler_params=_cparams(("parallel", "parallel")),
    )(c_pad, w_ada, b_ada.reshape(depth, 1, n6))


def _inproj_kernel(x_ref, mod_ref, w_ref, o_ref, h_ref):
    @pl.when(pl.program_id(1) == 0)
    def _():
        m = mod_ref[0]
        h_ref[...] = (x_ref[...] * (1.0 + m[1:2]) + m[0:1]).astype(BF16)

    o_ref[...] = _dot(h_ref[...], w_ref[...]).astype(o_ref.dtype)


def _inproj(x, mod, w, bm, bn, seg):
    n, d = x.shape
    nw = w.shape[1]
    return pl.pallas_call(
        _inproj_kernel,
        grid=(n // bm, nw // bn),
        in_specs=[pl.BlockSpec((bm, d), lambda i, j: (i, 0)),
                  pl.BlockSpec((1, 6, d), lambda i, j: ((i * bm) // seg, 0, 0)),
                  pl.BlockSpec((d, bn), lambda i, j: (0, j))],
        out_specs=pl.BlockSpec((bm, bn), lambda i, j: (i, j)),
        out_shape=jax.ShapeDtypeStruct((n, nw), BF16),
        scratch_shapes=[pltpu.VMEM((bm, d), BF16)],
        compiler_params=_cparams(("parallel", "arbitrary")),
    )(x, mod, w)


def _s5_operators(lam_re, lam_im, log_step, b_re, b_im, c_re, c_im):
    T = S5_CHUNK
    hi = lax.Precision.HIGHEST
    m_tot, w_cols, v_rows, a_re, a_im = 0.0, [], [], [], []
    for direction in range(2):
        lr = lam_re[direction].astype(F32)
        li = lam_im[direction].astype(F32)
        step = jnp.exp(log_step[direction].astype(F32))[:, None]
        mag = jnp.exp(lr * step)
        ar = mag * jnp.cos(li * step)
        ai = mag * jnp.sin(li * step)
        den = lr * lr + li * li
        fr = ((ar - 1.0) * lr + ai * li) / den
        fi = (ai * lr - (ar - 1.0) * li) / den
        br = b_re[direction].astype(F32)
        bi = b_im[direction].astype(F32)
        bbr = fr[..., None] * br - fi[..., None] * bi
        bbi = fr[..., None] * bi + fi[..., None] * br
        cr = c_re[direction].astype(F32)
        ci = c_im[direction].astype(F32)
        pr, pi = [jnp.ones_like(ar)], [jnp.zeros_like(ar)]
        for _ in range(T):
            pr.append(pr[-1] * ar - pi[-1] * ai)
            pi.append(pr[-2] * ai + pi[-1] * ar)
        pr = jnp.stack(pr)
        pi = jnp.stack(pi)
        abr = pr[:T, :, :, None] * bbr[None] - pi[:T, :, :, None] * bbi[None]
        abi = pr[:T, :, :, None] * bbi[None] + pi[:T, :, :, None] * bbr[None]
        kern = (jnp.einsum('ghp,tgpk->tghk', cr, abr, precision=hi)
                - jnp.einsum('ghp,tgpk->tghk', ci, abi, precision=hi))
        s_idx = jnp.arange(T)[:, None]
        t_idx = jnp.arange(T)[None, :]
        tau = (t_idx - s_idx) if direction == 0 else (s_idx - t_idx)
        ktab = kern[jnp.clip(tau, 0, T - 1)]
        ktab = jnp.where((tau >= 0)[:, :, None, None, None], ktab, 0.0)
        m_tot = m_tot + ktab.transpose(2, 0, 4, 1, 3).reshape(S5_GROUPS, T * S5_GROUP, T * S5_GROUP)
        order = jnp.arange(T - 1, -1, -1) if direction == 0 else jnp.arange(T)
        w_cols.append((abr[order].transpose(1, 0, 3, 2).reshape(S5_GROUPS, T * S5_GROUP, S5_STATE),
                       abi[order].transpose(1, 0, 3, 2).reshape(S5_GROUPS, T * S5_GROUP, S5_STATE)))
        k_idx = (jnp.arange(T) + 1) if direction == 0 else (T - jnp.arange(T))
        qr, qi = pr[k_idx], pi[k_idx]
        v_re = cr[None] * qr[:, :, None, :] - ci[None] * qi[:, :, None, :]
        v_im = -(cr[None] * qi[:, :, None, :] + ci[None] * qr[:, :, None, :])
        v_rows.append((v_re.transpose(1, 3, 0, 2).reshape(S5_GROUPS, S5_STATE, T * S5_GROUP),
                       v_im.transpose(1, 3, 0, 2).reshape(S5_GROUPS, S5_STATE, T * S5_GROUP)))
        a_re.append(pr[T])
        a_im.append(pi[T])
    w_op = jnp.concatenate([w_cols[0][0], w_cols[1][0], w_cols[0][1], w_cols[1][1]], axis=2)
    v_op = jnp.concatenate([v_rows[0][0], v_rows[1][0], v_rows[0][1], v_rows[1][1]], axis=1)
    a_op = jnp.stack([jnp.concatenate(a_re, axis=1), jnp.concatenate(a_im, axis=1)], axis=1)
    return m_tot.astype(BF16), w_op.astype(BF16), v_op.astype(BF16), a_op


GROUPS_PER_TILE = LANES // S5_GROUP
ROWS_PER_PACKED_VREG = 2 * SUBLANES


def _swap_atoms(tiles, n_out, n_in):
    atom = lax.broadcasted_iota(jnp.int32, tiles[0].shape, 1) // S5_GROUP
    out = []
    for o in range(n_out):
        acc = None
        for i in range(n_in):
            shift = ((i - o) % GROUPS_PER_TILE) * S5_GROUP
            piece = tiles[i] if shift == 0 else pltpu.roll(tiles[i], shift, 1)
            acc = piece if acc is None else jnp.where(atom == i, piece, acc)
        out.append(acc)
    return out


def _s5_core_kernel(*refs, shapes):
    T = S5_CHUNK
    u_refs, (w_ref, m_ref, v_ref, a_ref) = refs[:T], refs[T:T + 4]
    y_ref, x_ref, yo_ref, s0_ref, s1_ref, h0_ref, h1_ref = refs[T + 4:]
    rows = x_ref.shape[0]
    half = S5_STATE
    width = T * S5_GROUP
    halves = width // LANES

    def gather_rows(i, carry):
        r0 = pl.multiple_of(i * ROWS_PER_PACKED_VREG, ROWS_PER_PACKED_VREG)
        src = [pltpu.bitcast(u_refs[s][pl.ds(r0, ROWS_PER_PACKED_VREG), :], jnp.uint32) for s in range(T)]
        for k in range(halves):
            per_group = _swap_atoms(src[k * GROUPS_PER_TILE:(k + 1) * GROUPS_PER_TILE], GROUPS_PER_TILE, GROUPS_PER_TILE)
            for g in range(GROUPS_PER_TILE):
                lo = g * width + k * LANES
                x_ref[pl.ds(r0, ROWS_PER_PACKED_VREG), lo:lo + LANES] = pltpu.bitcast(per_group[g], BF16)
        return carry

    lax.fori_loop(0, rows // ROWS_PER_PACKED_VREG, gather_rows, 0)

    for g in range(GROUPS_PER_TILE):
        s = _dot(x_ref[:, g * width:(g + 1) * width], w_ref[g])
        s0_ref[pl.ds(g, rows, stride=GROUPS_PER_TILE), :] = s[:, :LANES]
        s1_ref[pl.ds(g, rows, stride=GROUPS_PER_TILE), :] = s[:, LANES:]

    is_fwd = lax.broadcasted_iota(jnp.int32, (SUBLANES, LANES), 1) < half
    a_re, a_im = a_ref[0, 0], a_ref[0, 1]

    def scan(batch, n_chunks):
        def body(i, carry):
            new = []
            for b in range(batch):
                rf = pl.multiple_of((b * n_chunks + i) * SUBLANES, SUBLANES)
                rb = pl.multiple_of((b * n_chunks + n_chunks - 1 - i) * SUBLANES, SUBLANES)
                h0, h1 = carry[b]
                h0_ref[pl.ds(rf, SUBLANES), 0:half] = h0[:, :half]
                h1_ref[pl.ds(rf, SUBLANES), 0:half] = h1[:, :half]
                h0_ref[pl.ds(rb, SUBLANES), half:LANES] = h0[:, half:]
                h1_ref[pl.ds(rb, SUBLANES), half:LANES] = h1[:, half:]
                s0 = jnp.where(is_fwd, s0_ref[pl.ds(rf, SUBLANES), :], s0_ref[pl.ds(rb, SUBLANES), :])
                s1 = jnp.where(is_fwd, s1_ref[pl.ds(rf, SUBLANES), :], s1_ref[pl.ds(rb, SUBLANES), :])
                new.append((a_re * h0 - a_im * h1 + s0, a_re * h1 + a_im * h0 + s1))
            return tuple(new)

        zero = jnp.zeros((SUBLANES, LANES), F32)
        lax.fori_loop(0, n_chunks, body, tuple((zero, zero) for _ in range(batch)))

    for idx, (batch, n_chunks) in enumerate(shapes):
        @pl.when(pl.program_id(0) == idx)
        def _(batch=batch, n_chunks=n_chunks):
            scan(batch, n_chunks)

    for g in range(GROUPS_PER_TILE):
        xg = x_ref[:, g * width:(g + 1) * width]
        hg = jnp.concatenate([h0_ref[pl.ds(g, rows, stride=GROUPS_PER_TILE), :],
                              h1_ref[pl.ds(g, rows, stride=GROUPS_PER_TILE), :]], axis=1).astype(BF16)
        y = _dot(xg, m_ref[g]) + _dot(hg, v_ref[g])
        yo_ref[:, g * width:(g + 1) * width] = y.astype(yo_ref.dtype)

    def scatter_rows(i, carry):
        r0 = pl.multiple_of(i * ROWS_PER_PACKED_VREG, ROWS_PER_PACKED_VREG)
        for k in range(halves):
            src = [pltpu.bitcast(yo_ref[pl.ds(r0, ROWS_PER_PACKED_VREG), g * width + k * LANES:g * width + (k + 1) * LANES],
                                 jnp.uint32) for g in range(GROUPS_PER_TILE)]
            per_time = _swap_atoms(src, GROUPS_PER_TILE, GROUPS_PER_TILE)
            for j in range(GROUPS_PER_TILE):
                y_ref[k * GROUPS_PER_TILE + j, pl.ds(r0, ROWS_PER_PACKED_VREG), :] = pltpu.bitcast(per_time[j], BF16)
        return carry

    lax.fori_loop(0, rows // ROWS_PER_PACKED_VREG, scatter_rows, 0)


def _s5_core(proj, w_op, m_op, v_op, a_op, shapes):
    T = S5_CHUNK
    n_tok, pw = proj.shape
    rows = shapes[0][0] * shapes[0][1]
    assert all(b * c == rows for b, c in shapes) and rows * T * len(shapes) == n_tok
    tiles_in = pw // LANES
    width = T * S5_GROUP
    u_view = proj.reshape(n_tok // T, T * pw)

    def in_map(s):
        return lambda r, o: (r, s * tiles_in + o)

    op_spec = pl.BlockSpec((GROUPS_PER_TILE, width, width), lambda r, o: (o, 0, 0))
    y = pl.pallas_call(
        functools.partial(_s5_core_kernel, shapes=tuple(shapes)),
        grid=(len(shapes), S5_GROUPS // GROUPS_PER_TILE),
        in_specs=[pl.BlockSpec((rows, LANES), in_map(s)) for s in range(T)]
        + [op_spec, op_spec, op_spec, pl.BlockSpec((1, 2, GROUPS_PER_TILE, LANES), lambda r, o: (o, 0, 0, 0))],
        out_specs=pl.BlockSpec((T, rows, LANES), lambda r, o: (0, r, o)),
        out_shape=jax.ShapeDtypeStruct((T, n_tok // T, S5_WIDTH), BF16),
        scratch_shapes=[pltpu.VMEM((rows, GROUPS_PER_TILE * width), BF16),
                        pltpu.VMEM((rows, GROUPS_PER_TILE * width), BF16),
                        ] + [pltpu.VMEM((GROUPS_PER_TILE * rows, LANES), F32)] * 4,
        compiler_params=_cparams(("parallel", "parallel")),
    )(*([u_view] * T), w_op, m_op, v_op,
      a_op.reshape(S5_GROUPS // GROUPS_PER_TILE, GROUPS_PER_TILE, 2, LANES).transpose(0, 2, 1, 3))
    return y.transpose(1, 0, 2).reshape(n_tok, S5_WIDTH)


def _s5_post_kernel(y_ref, u_ref, d_ref, wg_ref, bg_ref, z_ref):
    y = d_ref[...] * u_ref[...].astype(F32) + y_ref[...].astype(F32)
    act = 0.5 * y * (1.0 + jnp.tanh(math.sqrt(2.0 / math.pi) * (y + 0.044715 * (y * y * y))))
    gl = _dot(act.astype(BF16), wg_ref[...]) + bg_ref[...]
    z_ref[...] = (act * _sigmoid(gl)).astype(z_ref.dtype)


def _s5_post(y, proj, d, wg, bg, bm):
    n, w = y.shape
    return pl.pallas_call(
        _s5_post_kernel,
        grid=(n // bm,),
        in_specs=[pl.BlockSpec((bm, w), lambda i: (i, 0)),
                  pl.BlockSpec((bm, w), lambda i: (i, COL_U // S5_WIDTH)),
                  pl.BlockSpec((1, w), lambda i: (0, 0)),
                  pl.BlockSpec((w, w), lambda i: (0, 0)),
                  pl.BlockSpec((1, w), lambda i: (0, 0))],
        out_specs=pl.BlockSpec((bm, w), lambda i: (i, 0)),
        out_shape=jax.ShapeDtypeStruct((n, w), BF16),
        compiler_params=_cparams(("parallel",)),
    )(y, proj, d.reshape(1, w), wg, bg.reshape(1, w))


def _rms_norm(x, g):
    return x * lax.rsqrt(jnp.mean(x * x, axis=-1, keepdims=True) + RMS_EPS) * g


def _rope_block(r, cos, sin):
    return r * cos + pltpu.roll(r, LANES // 2, 1) * sin


def _qproj_kernel(cq_ref, g_ref, w_ref, cos_ref, sin_ref, q_ref, *, heads, scale):
    xn = _rms_norm(cq_ref[...].astype(F32), g_ref[...]).astype(BF16)
    acc = _dot(xn, w_ref[...])
    cos, sin = cos_ref[...], sin_ref[...]
    for h in range(heads):
        lo = h * HEAD_PAD
        q_ref[:, lo:lo + LANES] = (acc[:, lo:lo + LANES] * scale).astype(q_ref.dtype)
        rope = _rope_block(acc[:, lo + LANES:lo + HEAD_PAD], cos, sin)
        q_ref[:, lo + LANES:lo + HEAD_PAD] = (rope * scale).astype(q_ref.dtype)


def _qproj(proj, g, w, cos, sin, bm, pos_block, heads_per_step=4):
    n = proj.shape[0]
    nq = w.shape[1]
    bn = heads_per_step * HEAD_PAD
    scale = (QK_NOPE + QK_ROPE) ** -0.5 * math.log2(math.e)
    return pl.pallas_call(
        functools.partial(_qproj_kernel, heads=heads_per_step, scale=scale),
        grid=(n // bm, nq // bn),
        in_specs=[pl.BlockSpec((bm, Q_LORA), lambda i, j: (i, COL_CQ // Q_LORA)),
                  pl.BlockSpec((1, Q_LORA), lambda i, j: (0, 0)),
                  pl.BlockSpec((Q_LORA, bn), lambda i, j: (0, j)),
                  pl.BlockSpec((bm, LANES), lambda i, j: (pos_block(i), 0)),
                  pl.BlockSpec((bm, LANES), lambda i, j: (pos_block(i), 0))],
        out_specs=pl.BlockSpec((bm, bn), lambda i, j: (i, j)),
        out_shape=jax.ShapeDtypeStruct((n, nq), BF16),
        compiler_params=_cparams(("parallel", "parallel")),
    )(proj, g.reshape(1, Q_LORA), w, cos, sin)


def _kvproj_kernel(ckv_ref, kr_ref, g_ref, wk_ref, wv_ref, cos_ref, sin_ref, k_ref, v_ref):
    xn = _rms_norm(ckv_ref[...].astype(F32), g_ref[...]).astype(BF16)
    kn = _dot(xn, wk_ref[...])
    vv = _dot(xn, wv_ref[...])
    kro = _rope_block(kr_ref[...].astype(F32), cos_ref[...], sin_ref[...]).astype(k_ref.dtype)
    ones = (lax.broadcasted_iota(jnp.int32, kro.shape, 1) == 0).astype(v_ref.dtype)
    for h in range(MLA_HEADS):
        lo = h * HEAD_PAD
        k_ref[:, lo:lo + LANES] = kn[:, h * QK_NOPE:(h + 1) * QK_NOPE].astype(k_ref.dtype)
        k_ref[:, lo + LANES:lo + HEAD_PAD] = kro
        v_ref[:, 2 * h * V_HEAD:(2 * h + 1) * V_HEAD] = vv[:, h * V_HEAD:(h + 1) * V_HEAD].astype(v_ref.dtype)
        v_ref[:, (2 * h + 1) * V_HEAD:(2 * h + 2) * V_HEAD] = ones


def _kvproj(proj, g, wk, wv, cos, sin, bm, pos_block):
    n = proj.shape[0]
    return pl.pallas_call(
        _kvproj_kernel,
        grid=(n // bm,),
        in_specs=[pl.BlockSpec((bm, KV_LORA), lambda i: (i, COL_CKV // KV_LORA)),
                  pl.BlockSpec((bm, LANES), lambda i: (i, COL_KR // LANES)),
                  pl.BlockSpec((1, KV_LORA), lambda i: (0, 0)),
                  pl.BlockSpec(wk.shape, lambda i: (0, 0)),
                  pl.BlockSpec(wv.shape, lambda i: (0, 0)),
                  pl.BlockSpec((bm, LANES), lambda i: (pos_block(i), 0)),
                  pl.BlockSpec((bm, LANES), lambda i: (pos_block(i), 0))],
        out_specs=[pl.BlockSpec((bm, MLA_HEADS * HEAD_PAD), lambda i: (i, 0)),
                   pl.BlockSpec((bm, MLA_HEADS * 2 * V_HEAD), lambda i: (i, 0))],
        out_shape=[jax.ShapeDtypeStruct((n, MLA_HEADS * HEAD_PAD), BF16),
                   jax.ShapeDtypeStruct((n, MLA_HEADS * 2 * V_HEAD), BF16)],
        compiler_params=_cparams(("parallel",)),
    )(proj, proj, g.reshape(1, KV_LORA), wk, wv, cos, sin)


def _attn_kernel(q_ref, k_ref, v_ref, o_ref, *, n_chunks, tk):
    q = q_ref[...]

    def scores(c):
        return lax.dot_general(q, k_ref[c * tk:(c + 1) * tk, :], (((1,), (1,)), ((), ())),
                               preferred_element_type=F32)

    m = jnp.full((q.shape[0], 1), -jnp.inf, F32)
    acc = jnp.zeros((q.shape[0], 2 * V_HEAD), F32)
    s = scores(0)
    for c in range(n_chunks):
        s_next = scores(c + 1) if c + 1 < n_chunks else None
        m_new = jnp.maximum(m, jnp.max(s, axis=-1, keepdims=True))
        p = jnp.exp2((s - m_new).astype(BF16))
        acc = jnp.exp2(m - m_new) * acc + _dot(p, v_ref[c * tk:(c + 1) * tk, :])
        m, s = m_new, s_next
    o_ref[...] = (acc[:, :V_HEAD] / acc[:, V_HEAD:V_HEAD + 1]).astype(o_ref.dtype)


def _attention(q, k, v, base, batch, length, tq):
    nq = length // tq
    row0 = base // tq
    seq0 = base // length
    tk = min(1024, length)
    return pl.pallas_call(
        functools.partial(_attn_kernel, n_chunks=length // tk, tk=tk),
        grid=(batch, MLA_HEADS, nq),
        in_specs=[pl.BlockSpec((tq, HEAD_PAD), lambda b, h, i: (row0 + b * nq + i, h)),
                  pl.BlockSpec((length, HEAD_PAD), lambda b, h, i: (seq0 + b, h)),
                  pl.BlockSpec((length, 2 * V_HEAD), lambda b, h, i: (seq0 + b, h))],
        out_specs=pl.BlockSpec((tq, V_HEAD), lambda b, h, i: (b * nq + i, h)),
        out_shape=jax.ShapeDtypeStruct((batch * length, MLA_HEADS * V_HEAD), BF16),
        compiler_params=_cparams(("parallel", "parallel", "arbitrary")),
    )(q, k, v)


def _merge_kernel(z_ref, o1_ref, o2_ref, gs_ref, gm_ref, wsp_ref, wmp_ref, out_ref, *, split):
    ys = _dot(z_ref[...], wsp_ref[...])
    o = jnp.where(pl.program_id(0) < split, o1_ref[...], o2_ref[...])
    ym = _dot(o, wmp_ref[...])
    out = _sigmoid(gs_ref[...].astype(F32)) * ys + _sigmoid(gm_ref[...].astype(F32)) * ym
    out_ref[...] = out.astype(out_ref.dtype)


def _merge(z, o1, o2, proj, wsp, wmp, bm):
    n = z.shape[0]
    d = D_MODEL
    split = o1.shape[0] // bm
    return pl.pallas_call(
        functools.partial(_merge_kernel, split=split),
        grid=(n // bm,),
        in_specs=[pl.BlockSpec((bm, S5_WIDTH), lambda i: (i, 0)),
                  pl.BlockSpec((bm, d), lambda i: (jnp.minimum(i, split - 1), 0)),
                  pl.BlockSpec((bm, d), lambda i: (jnp.maximum(i - split, 0), 0)),
                  pl.BlockSpec((bm, d), lambda i: (i, COL_GS // d)),
                  pl.BlockSpec((bm, d), lambda i: (i, COL_GM // d)),
                  pl.BlockSpec(wsp.shape, lambda i: (0, 0)),
                  pl.BlockSpec(wmp.shape, lambda i: (0, 0))],
        out_specs=pl.BlockSpec((bm, d), lambda i: (i, 0)),
        out_shape=jax.ShapeDtypeStruct((n, d), BF16),
        compiler_params=_cparams(("parallel",)),
    )(z, o1, o2, proj, proj, wsp, wmp)


def _out_kernel(mg_ref, x_ref, mod_ref, wo_ref, lng_ref, lnb_ref, wr_ref, xo_ref, h2_ref, aff_ref):
    m = mod_ref[0]
    out = _dot(mg_ref[...], wo_ref[...])
    xn = _layer_norm(DEEPNORM_ALPHA * x_ref[...] + m[2:3] * out, lng_ref[...], lnb_ref[...])
    xo_ref[...] = xn
    h2 = (xn * (1.0 + m[4:5]) + m[3:4]).astype(BF16)
    h2_ref[...] = h2
    logits = _dot(h2, wr_ref[...])
    lane = lax.broadcasted_iota(jnp.int32, logits.shape, 1)
    logits = jnp.where(lane < N_EXPERTS, logits, -jnp.inf)
    e = jnp.exp(logits - jnp.max(logits, axis=-1, keepdims=True))
    aff_ref[...] = e / jnp.sum(e, axis=-1, keepdims=True)


def _out_proj(merged, x, mod, wo, lng, lnb, wr, bm, seg):
    n, d = x.shape
    return pl.pallas_call(
        _out_kernel,
        grid=(n // bm,),
        in_specs=[pl.BlockSpec((bm, d), lambda i: (i, 0)),
                  pl.BlockSpec((bm, d), lambda i: (i, 0)),
                  pl.BlockSpec((1, 6, d), lambda i: ((i * bm) // seg, 0, 0)),
                  pl.BlockSpec((d, d), lambda i: (0, 0)),
                  pl.BlockSpec((1, d), lambda i: (0, 0)),
                  pl.BlockSpec((1, d), lambda i: (0, 0)),
                  pl.BlockSpec((d, LANES), lambda i: (0, 0))],
        out_specs=[pl.BlockSpec((bm, d), lambda i: (i, 0)),
                   pl.BlockSpec((bm, d), lambda i: (i, 0)),
                   pl.BlockSpec((bm, LANES), lambda i: (i, 0))],
        out_shape=[jax.ShapeDtypeStruct((n, d), F32),
                   jax.ShapeDtypeStruct((n, d), BF16),
                   jax.ShapeDtypeStruct((n, LANES), F32)],
        compiler_params=_cparams(("parallel",)),
    )(merged, x, mod, wo, lng.reshape(1, d), lnb.reshape(1, d), wr)


def _select_kernel(aff_ref, gate_ref, cnt_ref, off_ref, *, n, cap, bm):
    aff = aff_ref[...]
    bits = pltpu.bitcast(aff, jnp.int32)
    idx = lax.broadcasted_iota(jnp.int32, aff.shape, 0)
    valid = lax.broadcasted_iota(jnp.int32, (1, LANES), 1) < N_EXPERTS

    def count(mask):
        return jnp.sum(mask.astype(jnp.int32), axis=0, keepdims=True)

    def thr_step(k, thr):
        cand = thr | jnp.left_shift(jnp.int32(1), 30 - k)
        return jnp.where(count(bits >= cand) >= cap, cand, thr)

    thr = lax.fori_loop(0, 31, thr_step, jnp.zeros((1, LANES), jnp.int32))
    above = bits > thr
    tied = bits == thr
    need = cap - count(above)
    idx_bits = max(1, (n - 1).bit_length())

    def idx_step(k, bound):
        cand = bound | jnp.left_shift(jnp.int32(1), idx_bits - 1 - k)
        return jnp.where(count(tied & (idx < cand)) < need, cand, bound)

    bound = lax.fori_loop(0, idx_bits, idx_step, jnp.zeros((1, LANES), jnp.int32))
    sel = (above | (tied & (idx <= bound))) & valid
    gate_ref[...] = jnp.where(sel, aff, -1.0)
    nt = n // bm
    cnt = jnp.sum(sel.astype(F32).reshape(nt, bm, LANES), axis=1)
    r = lax.broadcasted_iota(jnp.int32, (nt, nt), 0)
    c = lax.broadcasted_iota(jnp.int32, (nt, nt), 1)
    off = _dot((c < r).astype(BF16), cnt.astype(BF16))
    cnt_ref[0] = cnt.astype(jnp.int32)
    off_ref[0] = off.astype(jnp.int32)


def _select(aff, n_sets, n, cap, bm):
    nt = n // bm
    return pl.pallas_call(
        functools.partial(_select_kernel, n=n, cap=cap, bm=bm),
        grid=(n_sets,),
        in_specs=[pl.BlockSpec((n, LANES), lambda g: (g, 0))],
        out_specs=[pl.BlockSpec((n, LANES), lambda g: (g, 0)),
                   pl.BlockSpec((1, nt, LANES), lambda g: (g, 0, 0)),
                   pl.BlockSpec((1, nt, LANES), lambda g: (g, 0, 0))],
        out_shape=[jax.ShapeDtypeStruct((n_sets * n, LANES), F32),
                   jax.ShapeDtypeStruct((n_sets, nt, LANES), jnp.int32),
                   jax.ShapeDtypeStruct((n_sets, nt, LANES), jnp.int32)],
        compiler_params=_cparams(("parallel",)),
    )(aff)


def _slot_onehot(gate, pos, offs, win_lo, dma_lo, weighted):
    bm = gate.shape[0]
    j = lax.broadcasted_iota(jnp.int32, (bm, MOE_SLAB), 1).astype(F32)
    cols = []
    for e in range(N_EXPERTS):
        ge = gate[:, e:e + 1]
        slot = pos[:, e:e + 1] + offs[e].astype(F32)
        lo = win_lo[e].astype(F32)
        hit = (ge >= 0.0) & (slot >= lo) & (slot < lo + MOE_SLAB) & ((slot - dma_lo[e].astype(F32)) == j)
        cols.append(jnp.where(hit, ge if weighted else 1.0, 0.0).astype(BF16))
    return jnp.concatenate(cols, axis=1)


def _slot_positions(gate):
    bm = gate.shape[0]
    r = lax.broadcasted_iota(jnp.int32, (bm, bm), 0)
    c = lax.broadcasted_iota(jnp.int32, (bm, bm), 1)
    return _dot((c < r).astype(BF16), (gate >= 0.0).astype(BF16))


def _align_down(v):
    return (v // SLOT_ALIGN) * SLOT_ALIGN


def _aligned(v):
    return v if isinstance(v, int) else pl.multiple_of(v, SLOT_ALIGN)


def _compact_kernel(off_ref, npass_ref, coff_ref, nsub_ref, h_ref, gate_ref, xe_hbm,
                    slab_ref, carry_ref, sem, pending_ref, *, nt, cap, n_steps):
    g = pl.program_id(0)
    i = pl.program_id(1)
    step = g * nt + i
    buf = step % 2
    base = step * N_EXPERTS
    offs = [off_ref[base + e] for e in range(N_EXPERTS)]
    npass = [npass_ref[base + e] for e in range(N_EXPERTS)]

    def window_copy(which, e, lo):
        return pltpu.make_async_copy(
            slab_ref.at[which, pl.ds(e * MOE_SLAB, MOE_SLAB)],
            xe_hbm.at[g, e, pl.ds(_aligned(lo), MOE_SLAB)], sem.at[which, e])

    def drain(which):
        for e in range(N_EXPERTS):
            window_copy(which, e, 0).wait()

    @pl.when(step == 0)
    def _():
        pending_ref[0] = 0

    @pl.when(i == 0)
    def _():
        carry_ref[...] = jnp.zeros_like(carry_ref)
        slab_ref[buf] = jnp.zeros(slab_ref.shape[1:], slab_ref.dtype)
        for e in range(N_EXPERTS):
            window_copy(buf, e, cap).start()
        drain(buf)

    gate = gate_ref[...]
    pos = _slot_positions(gate)

    def build(s):
        lo = [_align_down(offs[e]) + s * MOE_SLAB for e in range(N_EXPERTS)]
        sel_t = _slot_onehot(gate, pos, offs, lo, lo, weighted=False)
        slab = lax.dot_general(sel_t, h_ref[...], (((0,), (0,)), ((), ())), preferred_element_type=F32)
        slab_ref[buf] = slab.astype(slab_ref.dtype)
        return lo

    def keep_tail(s):
        for e in range(N_EXPERTS):
            @pl.when(s == npass[e] - 1)
            def _(e=e):
                src = pl.multiple_of(e * MOE_SLAB + coff_ref[base + e], SLOT_ALIGN)
                carry_ref[pl.ds(e * SLOT_ALIGN, SLOT_ALIGN), :] = slab_ref[buf, pl.ds(src, SLOT_ALIGN), :]

    lo0 = build(0)
    for e in range(N_EXPERTS):
        rows = pl.ds(e * MOE_SLAB, SLOT_ALIGN)
        merged = slab_ref[buf, rows, :].astype(F32) + carry_ref[pl.ds(e * SLOT_ALIGN, SLOT_ALIGN), :].astype(F32)
        slab_ref[buf, rows, :] = merged.astype(slab_ref.dtype)
    keep_tail(0)

    @pl.when(pending_ref[0] == 1)
    def _():
        drain(1 - buf)

    for e in range(N_EXPERTS):
        window_copy(buf, e, lo0[e]).start()
    pending_ref[0] = 1

    n_pass = nsub_ref[step]

    @pl.when((n_pass > 1) | (step == n_steps - 1))
    def _():
        drain(buf)
        pending_ref[0] = 0

    def extra_pass(s, carry):
        lo = build(s)
        keep_tail(s)
        for e in range(N_EXPERTS):
            @pl.when(s < npass[e])
            def _(e=e):
                window_copy(buf, e, lo[e]).start()
        for e in range(N_EXPERTS):
            @pl.when(s < npass[e])
            def _(e=e):
                window_copy(buf, e, lo[e]).wait()
        return carry

    lax.fori_loop(1, n_pass, extra_pass, 0)


def _compact(off, npass, coff, nsub, h2, gate, n_sets, n, cap, bm):
    nt = n // bm
    d = h2.shape[1]
    grid_spec = pltpu.PrefetchScalarGridSpec(
        num_scalar_prefetch=4,
        grid=(n_sets, nt),
        in_specs=[pl.BlockSpec((bm, d), lambda g, i, *_: (g * nt + i, 0)),
                  pl.BlockSpec((bm, LANES), lambda g, i, *_: (g * nt + i, 0))],
        out_specs=pl.BlockSpec(memory_space=pl.ANY),
        scratch_shapes=[pltpu.VMEM((2, N_EXPERTS * MOE_SLAB, d), BF16),
                        pltpu.VMEM((N_EXPERTS * SLOT_ALIGN, d), BF16),
                        pltpu.SemaphoreType.DMA((2, N_EXPERTS)),
                        pltpu.SMEM((1,), jnp.int32)],
    )
    return pl.pallas_call(
        functools.partial(_compact_kernel, nt=nt, cap=cap, n_steps=n_sets * nt),
        grid_spec=grid_spec,
        out_shape=jax.ShapeDtypeStruct((n_sets, N_EXPERTS, cap + MOE_SLAB, d), BF16),
        compiler_params=_cparams(("arbitrary", "arbitrary")),
    )(off, npass, coff, nsub, h2, gate)


def _ffn_kernel(xe_ref, w1_ref, w3_ref, w2_ref, ye_ref):
    x = xe_ref[0, 0]
    a = _dot(x, w1_ref[0])
    b = _dot(x, w3_ref[0])
    hid = (a * _sigmoid(a) * b).astype(BF16)
    ye_ref[0, 0] = _dot(hid, w2_ref[0]).astype(ye_ref.dtype)


def _ffn(xe, w1, w3, w2, n_sets, cap, tm):
    d = xe.shape[-1]
    f = w1.shape[-1]
    return pl.pallas_call(
        _ffn_kernel,
        grid=(N_EXPERTS, n_sets, cap // tm),
        in_specs=[pl.BlockSpec((1, 1, tm, d), lambda e, g, t: (g, e, t, 0)),
                  pl.BlockSpec((1, d, f), lambda e, g, t: (e, 0, 0)),
                  pl.BlockSpec((1, d, f), lambda e, g, t: (e, 0, 0)),
                  pl.BlockSpec((1, f, d), lambda e, g, t: (e, 0, 0))],
        out_specs=pl.BlockSpec((1, 1, tm, d), lambda e, g, t: (g, e, t, 0)),
        out_shape=jax.ShapeDtypeStruct((n_sets, N_EXPERTS, cap, d), BF16),
        compiler_params=_cparams(("parallel", "parallel", "parallel")),
    )(xe, w1, w3, w2)


def _combine_kernel(off_ref, nsub_ref, gate_ref, x_ref, mod_ref, lng_ref, lnb_ref, ye_hbm,
                    xo_ref, slab_ref, sem, *, nt, cap, n_steps):
    g = pl.program_id(0)
    i = pl.program_id(1)
    step = g * nt + i
    buf = step % 2
    gate = gate_ref[...]
    pos = _slot_positions(gate)

    def windows(at_step, s):
        offs = [off_ref[at_step * N_EXPERTS + e] for e in range(N_EXPERTS)]
        lo = [_align_down(offs[e]) + s * MOE_SLAB for e in range(N_EXPERTS)]
        start = [jnp.minimum(lo[e], cap - MOE_SLAB) for e in range(N_EXPERTS)]
        return offs, lo, start

    def fetch(which, at_step, s):
        _, _, start = windows(at_step, s)
        set_idx = at_step // nt
        return [pltpu.make_async_copy(
            ye_hbm.at[set_idx, e, pl.ds(_aligned(start[e]), MOE_SLAB)],
            slab_ref.at[which, pl.ds(e * MOE_SLAB, MOE_SLAB)], sem.at[which, e]) for e in range(N_EXPERTS)]

    def spread_dot(which, s):
        offs, lo, start = windows(step, s)
        return _dot(_slot_onehot(gate, pos, offs, lo, start, weighted=True), slab_ref[which])

    @pl.when(step == 0)
    def _():
        for cp in fetch(buf, step, 0):
            cp.start()

    @pl.when(step + 1 < n_steps)
    def _():
        for cp in fetch(1 - buf, step + 1, 0):
            cp.start()

    for cp in fetch(buf, step, 0):
        cp.wait()
    moe = spread_dot(buf, 0)

    def extra_pass(s, acc):
        copies = fetch(buf, step, s)
        for cp in copies:
            cp.start()
        for cp in copies:
            cp.wait()
        return acc + spread_dot(buf, s)

    moe = lax.fori_loop(1, nsub_ref[step], extra_pass, moe)
    m = mod_ref[0]
    xo_ref[...] = _layer_norm(DEEPNORM_ALPHA * x_ref[...] + m[5:6] * moe, lng_ref[...], lnb_ref[...])


def _combine(off, nsub, gate, x, mod, lng, lnb, ye, n_sets, n, cap, bm, seg):
    nt = n // bm
    d = x.shape[1]
    grid_spec = pltpu.PrefetchScalarGridSpec(
        num_scalar_prefetch=2,
        grid=(n_sets, nt),
        in_specs=[pl.BlockSpec((bm, LANES), lambda g, i, *_: (g * nt + i, 0)),
                  pl.BlockSpec((bm, d), lambda g, i, *_: (g * nt + i, 0)),
                  pl.BlockSpec((1, 6, d), lambda g, i, *_: (((g * nt + i) * bm) // seg, 0, 0)),
                  pl.BlockSpec((1, d), lambda g, i, *_: (0, 0)),
                  pl.BlockSpec((1, d), lambda g, i, *_: (0, 0)),
                  pl.BlockSpec(memory_space=pl.ANY)],
        out_specs=pl.BlockSpec((bm, d), lambda g, i, *_: (g * nt + i, 0)),
        scratch_shapes=[pltpu.VMEM((2, N_EXPERTS * MOE_SLAB, d), BF16),
                        pltpu.SemaphoreType.DMA((2, N_EXPERTS))],
    )
    return pl.pallas_call(
        functools.partial(_combine_kernel, nt=nt, cap=cap, n_steps=n_sets * nt),
        grid_spec=grid_spec,
        out_shape=jax.ShapeDtypeStruct(x.shape, F32),
        compiler_params=_cparams(("arbitrary", "arbitrary")),
    )(off, nsub, gate, x, mod, lng.reshape(1, d), lnb.reshape(1, d), ye)


def _rope_lanes(w):
    z = jnp.zeros(w.shape[:-1] + (ROPE_HALF,), w.dtype)
    return jnp.concatenate([w[..., :ROPE_HALF], z, w[..., ROPE_HALF:], z], axis=-1)


def _proj_weight(w_in):
    c0 = S5_WIDTH + Q_LORA + KV_LORA
    kr = w_in[:, c0:c0 + QK_ROPE]
    gates = w_in[:, c0 + QK_ROPE:]
    return jnp.concatenate([w_in[:, :c0], gates, _rope_lanes(kr)], axis=1).astype(BF16)


def _q_weight(w_uq):
    w = w_uq.reshape(Q_LORA, MLA_HEADS, QK_NOPE + QK_ROPE)
    w = jnp.concatenate([w[..., :QK_NOPE], _rope_lanes(w[..., QK_NOPE:])], axis=-1)
    return w.reshape(Q_LORA, MLA_HEADS * HEAD_PAD).astype(BF16)


def _kv_weights(w_ukv):
    w = w_ukv.reshape(KV_LORA, MLA_HEADS, QK_NOPE + V_HEAD)
    wk = w[..., :QK_NOPE].reshape(KV_LORA, MLA_HEADS * QK_NOPE)
    wv = w[..., QK_NOPE:].reshape(KV_LORA, MLA_HEADS * V_HEAD)
    return wk.astype(BF16), wv.astype(BF16)


def _rope_tables(length):
    pos = jnp.arange(length, dtype=F32)
    inv = 1.0 / (ROPE_BASE ** (jnp.arange(0, QK_ROPE, 2, dtype=F32) / QK_ROPE))
    ang = pos[:, None] * inv[None, :]
    cos, sin = jnp.cos(ang), jnp.sin(ang)
    z = jnp.zeros_like(cos)
    return (jnp.concatenate([cos, z, cos, z], axis=1), jnp.concatenate([-sin, z, sin, z], axis=1))


def _trunk(x_groups, c_groups, p):
    (b1, l1, d), (b2, l2, _) = x_groups[0].shape, x_groups[1].shape
    n1, n2 = b1 * l1, b2 * l2
    n = n1 + n2
    seg = l1
    assert l2 % l1 == 0 and n1 == n2 and n1 % l2 == 0
    cap = EC_CAPACITY * n1 // N_EXPERTS
    bm = min(512, seg)
    bm_in = min(1024, seg)
    bm_moe = min(256, seg)
    tq = min(512, l1)
    tm = min(256, cap)
    assert cap >= MOE_SLAB and cap % tm == 0

    x = jnp.concatenate([x_groups[0].reshape(n1, d), x_groups[1].reshape(n2, d)], axis=0)
    x = _ln_in(x, p['ln_in_g'], p['ln_in_b'], bm)

    c = jnp.concatenate(list(c_groups), axis=0)
    c_rows = -(-c.shape[0] // SUBLANES) * SUBLANES
    mod_all = _ada(jnp.pad(c, ((0, c_rows - c.shape[0]), (0, 0))), p['w_ada'], p['b_ada'])
    seg_rows = jnp.array(list(range(b1)) + [b1 + b for b in range(b2) for _ in range(l2 // seg)])
    mod_all = mod_all.reshape(DEPTH, c_rows, 6, d)[:, seg_rows]

    cos, sin = _rope_tables(max(l1, l2))

    def pos_block(bm_):
        def f(i):
            return jnp.where(i < n1 // bm_, i % (l1 // bm_), (i - n1 // bm_) % (l2 // bm_))
        return f

    for l in range(DEPTH):
        mod = mod_all[l]
        proj = _inproj(x, mod, _proj_weight(p['w_in'][l]), bm_in, PROJ_W // 7, seg)

        m_op, w_op, v_op, a_op = _s5_operators(p['s5_lam_re'][l], p['s5_lam_im'][l], p['s5_log_step'][l],
                                               p['s5_b_re'][l], p['s5_b_im'][l], p['s5_c_re'][l], p['s5_c_im'][l])
        y_s5 = _s5_core(proj, w_op, m_op, v_op, a_op, ((b1, l1 // S5_CHUNK), (b2, l2 // S5_CHUNK)))
        z = _s5_post(y_s5, proj, p['s5_d'][l], p['s5_w_glu'][l].astype(BF16),
                     p['s5_b_glu'][l], bm)

        q = _qproj(proj, p['mla_q_norm'][l], _q_weight(p['mla_w_uq'][l]), cos, sin, bm, pos_block(bm))
        wk, wv = _kv_weights(p['mla_w_ukv'][l])
        k, v = _kvproj(proj, p['mla_kv_norm'][l], wk, wv, cos, sin, bm, pos_block(bm))
        o1 = _attention(q, k, v, 0, b1, l1, tq)
        o2 = _attention(q, k, v, n1, b2, l2, tq)

        merged = _merge(z, o1, o2, proj, p['w_s5_proj'][l].astype(BF16), p['w_mla_proj'][l].astype(BF16), bm)
        wr = jnp.pad(p['w_router'][l], ((0, 0), (0, LANES - N_EXPERTS))).astype(BF16)
        x, h2, aff = _out_proj(merged, x, mod, p['w_o'][l].astype(BF16), p['ln1_g'][l], p['ln1_b'][l],
                               wr, bm, seg)

        gate, cnt, off = _select(aff, 2, n1, cap, bm_moe)
        cnt, off = cnt[..., :N_EXPERTS], off[..., :N_EXPERTS]
        filled = off % SLOT_ALIGN + cnt
        npass = filled // MOE_SLAB + 1
        coff = (filled // SLOT_ALIGN) * SLOT_ALIGN - (npass - 1) * MOE_SLAB
        nsub = jnp.max(npass, axis=-1).reshape(-1)
        off, npass, coff = off.reshape(-1), npass.reshape(-1), coff.reshape(-1)
        xe = _compact(off, npass, coff, nsub, h2, gate, 2, n1, cap, bm_moe)
        ye = _ffn(xe, p['w_exp1'][l].astype(BF16), p['w_exp3'][l].astype(BF16),
                  p['w_exp2'][l].astype(BF16), 2, cap, tm)
        x = _combine(off, nsub, gate, x, mod, p['ln2_g'][l], p['ln2_b'][l], ye, 2, n1, cap, bm_moe, seg)

    return x[:n1].reshape(b1, l1, d), x[n1:].reshape(b2, l2, d)


def kernel(x_prompt, x_sample, c_prompt, c_sample, ln_in_g, ln_in_b, w_ada, b_ada, w_in, s5_lam_re, s5_lam_im, s5_log_step, s5_b_re, s5_b_im, s5_c_re, s5_c_im, s5_d, s5_w_glu, s5_b_glu, w_s5_proj, mla_q_norm, mla_w_uq, mla_kv_norm, mla_w_ukv, w_mla_proj, w_o, ln1_g, ln1_b, w_router, w_exp1, w_exp3, w_exp2, ln2_g, ln2_b):
    params = dict(ln_in_g=ln_in_g, ln_in_b=ln_in_b, w_ada=w_ada, b_ada=b_ada, w_in=w_in,
                  s5_lam_re=s5_lam_re, s5_lam_im=s5_lam_im, s5_log_step=s5_log_step,
                  s5_b_re=s5_b_re, s5_b_im=s5_b_im, s5_c_re=s5_c_re, s5_c_im=s5_c_im, s5_d=s5_d,
                  s5_w_glu=s5_w_glu, s5_b_glu=s5_b_glu, w_s5_proj=w_s5_proj,
                  mla_q_norm=mla_q_norm, mla_w_uq=mla_w_uq, mla_kv_norm=mla_kv_norm,
                  mla_w_ukv=mla_w_ukv, w_mla_proj=w_mla_proj, w_o=w_o, ln1_g=ln1_g, ln1_b=ln1_b,
                  w_router=w_router, w_exp1=w_exp1, w_exp3=w_exp3, w_exp2=w_exp2,
                  ln2_g=ln2_g, ln2_b=ln2_b)
    return _trunk((x_prompt, x_sample), (c_prompt, c_sample), params)
```

```python
import functools
import math

import jax
import jax.numpy as jnp
from jax import lax
from jax.experimental import pallas as pl
from jax.experimental.pallas import tpu as pltpu

F32 = jnp.float32
BF16 = jnp.bfloat16

D_MODEL = 2048
DEPTH = 4
S5_WIDTH = D_MODEL // 2
S5_GROUP = 16
S5_GROUPS = S5_WIDTH // S5_GROUP
S5_STATE = 64
MLA_HEADS = 16
QK_NOPE = 128
QK_ROPE = 64
V_HEAD = 128
Q_LORA = 512
KV_LORA = 512
ROPE_BASE = 10000.0
N_EXPERTS = 16
EC_CAPACITY = 2
D_EXPERT = 1408
DEEPNORM_ALPHA = (2 * DEPTH) ** 0.25
LN_EPS = 1e-5
RMS_EPS = 1e-6

LANES = 128
SUBLANES = 8
MXU_DIM = 256
S5_CHUNK = MXU_DIM // S5_GROUP
HEAD_PAD = 2 * LANES
ROPE_HALF = QK_ROPE // 2
COL_U, COL_CQ, COL_CKV = 0, S5_WIDTH, S5_WIDTH + Q_LORA
COL_GS = S5_WIDTH + Q_LORA + KV_LORA
COL_GM = COL_GS + D_MODEL
COL_KR = COL_GM + D_MODEL
PROJ_W = COL_KR + LANES
MOE_SLAB = 64
SLOT_ALIGN = 16
VMEM_LIMIT = 56 * 1024 * 1024


def _cparams(sem, vmem=VMEM_LIMIT):
    return pltpu.CompilerParams(dimension_semantics=sem, vmem_limit_bytes=vmem)


def _dot(a, b):
    return jnp.dot(a, b, preferred_element_type=F32)


def _layer_norm(y, g, b):
    mu = jnp.mean(y, axis=-1, keepdims=True)
    yc = y - mu
    var = jnp.mean(yc * yc, axis=-1, keepdims=True)
    return yc * lax.rsqrt(var + LN_EPS) * g + b


def _sigmoid(x):
    return 1.0 / (1.0 + jnp.exp(-x))


def _ln_in_kernel(x_ref, g_ref, b_ref, o_ref):
    o_ref[...] = _layer_norm(x_ref[...], g_ref[...], b_ref[...])


def _ln_in(x, g, b, bm):
    n, d = x.shape
    return pl.pallas_call(
        _ln_in_kernel,
        grid=(n // bm,),
        in_specs=[pl.BlockSpec((bm, d), lambda i: (i, 0)),
                  pl.BlockSpec((1, d), lambda i: (0, 0)),
                  pl.BlockSpec((1, d), lambda i: (0, 0))],
        out_specs=pl.BlockSpec((bm, d), lambda i: (i, 0)),
        out_shape=jax.ShapeDtypeStruct((n, d), F32),
        compiler_params=_cparams(("parallel",)),
    )(x, g.reshape(1, d), b.reshape(1, d))


def _ada_kernel(c_ref, w_ref, b_ref, o_ref):
    c = c_ref[...]
    cond = (c * _sigmoid(c)).astype(BF16)
    o_ref[0] = _dot(cond, w_ref[0].astype(BF16)) + b_ref[0]


def _ada(c_pad, w_ada, b_ada, bn=1024):
    depth, d, n6 = w_ada.shape
    rows = c_pad.shape[0]
    return pl.pallas_call(
        _ada_kernel,
        grid=(depth, n6 // bn),
        in_specs=[pl.BlockSpec((rows, d), lambda l, j: (0, 0)),
                  pl.BlockSpec((1, d, bn), lambda l, j: (l, 0, j)),
                  pl.BlockSpec((1, 1, bn), lambda l, j: (l, 0, j))],
        out_specs=pl.BlockSpec((1, rows, bn), lambda l, j: (l, 0, j)),
        out_shape=jax.ShapeDtypeStruct((depth, rows, n6), F32),
        compiler_params=_cparams(("parallel", "parallel")),
    )(c_pad, w_ada, b_ada.reshape(depth, 1, n6))


def _inproj_kernel(x_ref, mod_ref, w_ref, o_ref, h_ref):
    @pl.when(pl.program_id(1) == 0)
    def _():
        m = mod_ref[0]
        h_ref[...] = (x_ref[...] * (1.0 + m[1:2]) + m[0:1]).astype(BF16)

    o_ref[...] = _dot(h_ref[...], w_ref[...]).astype(o_ref.dtype)


def _inproj(x, mod, w, bm, bn, seg):
    n, d = x.shape
    nw = w.shape[1]
    return pl.pallas_call(
        _inproj_kernel,
        grid=(n // bm, nw // bn),
        in_specs=[pl.BlockSpec((bm, d), lambda i, j: (i, 0)),
                  pl.BlockSpec((1, 6, d), lambda i, j: ((i * bm) // seg, 0, 0)),
                  pl.BlockSpec((d, bn), lambda i, j: (0, j))],
        out_specs=pl.BlockSpec((bm, bn), lambda i, j: (i, j)),
        out_shape=jax.ShapeDtypeStruct((n, nw), BF16),
        scratch_shapes=[pltpu.VMEM((bm, d), BF16)],
        compiler_params=_cparams(("parallel", "arbitrary")),
    )(x, mod, w)


def _s5_operators(lam_re, lam_im, log_step, b_re, b_im, c_re, c_im):
    T = S5_CHUNK
    hi = lax.Precision.HIGHEST
    m_tot, w_cols, v_rows, a_re, a_im = 0.0, [], [], [], []
    for direction in range(2):
        lr = lam_re[direction].astype(F32)
        li = lam_im[direction].astype(F32)
        step = jnp.exp(log_step[direction].astype(F32))[:, None]
        mag = jnp.exp(lr * step)
        ar = mag * jnp.cos(li * step)
        ai = mag * jnp.sin(li * step)
        den = lr * lr + li * li
        fr = ((ar - 1.0) * lr + ai * li) / den
        fi = (ai * lr - (ar - 1.0) * li) / den
        br = b_re[direction].astype(F32)
        bi = b_im[direction].astype(F32)
        bbr = fr[..., None] * br - fi[..., None] * bi
        bbi = fr[..., None] * bi + fi[..., None] * br
        cr = c_re[direction].astype(F32)
        ci = c_im[direction].astype(F32)
        pr, pi = [jnp.ones_like(ar)], [jnp.zeros_like(ar)]
        for _ in range(T):
            pr.append(pr[-1] * ar - pi[-1] * ai)
            pi.append(pr[-2] * ai + pi[-1] * ar)
        pr = jnp.stack(pr)
        pi = jnp.stack(pi)
        abr = pr[:T, :, :, None] * bbr[None] - pi[:T, :, :, None] * bbi[None]
        abi = pr[:T, :, :, None] * bbi[None] + pi[:T, :, :, None] * bbr[None]
        kern = (jnp.einsum('ghp,tgpk->tghk', cr, abr, precision=hi)
                - jnp.einsum('ghp,tgpk->tghk', ci, abi, precision=hi))
        s_idx = jnp.arange(T)[:, None]
        t_idx = jnp.arange(T)[None, :]
        tau = (t_idx - s_idx) if direction == 0 else (s_idx - t_idx)
        ktab = kern[jnp.clip(tau, 0, T - 1)]
        ktab = jnp.where((tau >= 0)[:, :, None, None, None], ktab, 0.0)
        m_tot = m_tot + ktab.transpose(2, 0, 4, 1, 3).reshape(S5_GROUPS, T * S5_GROUP, T * S5_GROUP)
        order = jnp.arange(T - 1, -1, -1) if direction == 0 else jnp.arange(T)
        w_cols.append((abr[order].transpose(1, 0, 3, 2).reshape(S5_GROUPS, T * S5_GROUP, S5_STATE),
                       abi[order].transpose(1, 0, 3, 2).reshape(S5_GROUPS, T * S5_GROUP, S5_STATE)))
        k_idx = (jnp.arange(T) + 1) if direction == 0 else (T - jnp.arange(T))
        qr, qi = pr[k_idx], pi[k_idx]
        v_re = cr[None] * qr[:, :, None, :] - ci[None] * qi[:, :, None, :]
        v_im = -(cr[None] * qi[:, :, None, :] + ci[None] * qr[:, :, None, :])
        v_rows.append((v_re.transpose(1, 3, 0, 2).reshape(S5_GROUPS, S5_STATE, T * S5_GROUP),
                       v_im.transpose(1, 3, 0, 2).reshape(S5_GROUPS, S5_STATE, T * S5_GROUP)))
        a_re.append(pr[T])
        a_im.append(pi[T])
    w_op = jnp.concatenate([w_cols[0][0], w_cols[1][0], w_cols[0][1], w_cols[1][1]], axis=2)
    v_op = jnp.concatenate([v_rows[0][0], v_rows[1][0], v_rows[0][1], v_rows[1][1]], axis=1)
    a_op = jnp.stack([jnp.concatenate(a_re, axis=1), jnp.concatenate(a_im, axis=1)], axis=1)
    return m_tot.astype(BF16), w_op.astype(BF16), v_op.astype(BF16), a_op


GROUPS_PER_TILE = LANES // S5_GROUP
ROWS_PER_PACKED_VREG = 2 * SUBLANES


def _swap_atoms(tiles, n_out, n_in):
    atom = lax.broadcasted_iota(jnp.int32, tiles[0].shape, 1) // S5_GROUP
    out = []
    for o in range(n_out):
        acc = None
        for i in range(n_in):
            shift = ((i - o) % GROUPS_PER_TILE) * S5_GROUP
            piece = tiles[i] if shift == 0 else pltpu.roll(tiles[i], shift, 1)
            acc = piece if acc is None else jnp.where(atom == i, piece, acc)
        out.append(acc)
    return out


def _s5_core_kernel(*refs, shapes):
    T = S5_CHUNK
    u_refs, (w_ref, m_ref, v_ref, a_ref, d_ref) = refs[:T], refs[T:T + 5]
    y_ref, x_ref, yo_ref, s0_ref, s1_ref, h0_ref, h1_ref = refs[T + 5:]
    rows = x_ref.shape[0]
    half = S5_STATE
    width = T * S5_GROUP
    halves = width // LANES

    def gather_rows(i, carry):
        r0 = pl.multiple_of(i * ROWS_PER_PACKED_VREG, ROWS_PER_PACKED_VREG)
        src = [pltpu.bitcast(u_refs[s][pl.ds(r0, ROWS_PER_PACKED_VREG), :], jnp.uint32) for s in range(T)]
        for k in range(halves):
            per_group = _swap_atoms(src[k * GROUPS_PER_TILE:(k + 1) * GROUPS_PER_TILE], GROUPS_PER_TILE, GROUPS_PER_TILE)
            for g in range(GROUPS_PER_TILE):
                lo = g * width + k * LANES
                x_ref[pl.ds(r0, ROWS_PER_PACKED_VREG), lo:lo + LANES] = pltpu.bitcast(per_group[g], BF16)
        return carry

    lax.fori_loop(0, rows // ROWS_PER_PACKED_VREG, gather_rows, 0)

    for g in range(GROUPS_PER_TILE):
        s = _dot(x_ref[:, g * width:(g + 1) * width], w_ref[g])
        s0_ref[pl.ds(g, rows, stride=GROUPS_PER_TILE), :] = s[:, :LANES]
        s1_ref[pl.ds(g, rows, stride=GROUPS_PER_TILE), :] = s[:, LANES:]

    is_fwd = lax.broadcasted_iota(jnp.int32, (SUBLANES, LANES), 1) < half
    a_re, a_im = a_ref[0, 0], a_ref[0, 1]

    def scan(batch, n_chunks):
        def body(i, carry):
            new = []
            for b in range(batch):
                rf = pl.multiple_of((b * n_chunks + i) * SUBLANES, SUBLANES)
                rb = pl.multiple_of((b * n_chunks + n_chunks - 1 - i) * SUBLANES, SUBLANES)
                h0, h1 = carry[b]
                h0_ref[pl.ds(rf, SUBLANES), 0:half] = h0[:, :half]
                h1_ref[pl.ds(rf, SUBLANES), 0:half] = h1[:, :half]
                h0_ref[pl.ds(rb, SUBLANES), half:LANES] = h0[:, half:]
                h1_ref[pl.ds(rb, SUBLANES), half:LANES] = h1[:, half:]
                s0 = jnp.where(is_fwd, s0_ref[pl.ds(rf, SUBLANES), :], s0_ref[pl.ds(rb, SUBLANES), :])
                s1 = jnp.where(is_fwd, s1_ref[pl.ds(rf, SUBLANES), :], s1_ref[pl.ds(rb, SUBLANES), :])
                new.append((a_re * h0 - a_im * h1 + s0, a_re * h1 + a_im * h0 + s1))
            return tuple(new)

        zero = jnp.zeros((SUBLANES, LANES), F32)
        lax.fori_loop(0, n_chunks, body, tuple((zero, zero) for _ in range(batch)))

    for idx, (batch, n_chunks) in enumerate(shapes):
        @pl.when(pl.program_id(0) == idx)
        def _(batch=batch, n_chunks=n_chunks):
            scan(batch, n_chunks)

    for g in range(GROUPS_PER_TILE):
        xg = x_ref[:, g * width:(g + 1) * width]
        hg = jnp.concatenate([h0_ref[pl.ds(g, rows, stride=GROUPS_PER_TILE), :],
                              h1_ref[pl.ds(g, rows, stride=GROUPS_PER_TILE), :]], axis=1).astype(BF16)
        y = _dot(xg, m_ref[g]) + _dot(hg, v_ref[g])
        yo_ref[:, g * width:(g + 1) * width] = y.astype(yo_ref.dtype)

    def scatter_rows(i, carry):
        r0 = pl.multiple_of(i * ROWS_PER_PACKED_VREG, ROWS_PER_PACKED_VREG)
        for k in range(halves):
            src = [pltpu.bitcast(yo_ref[pl.ds(r0, ROWS_PER_PACKED_VREG), g * width + k * LANES:g * width + (k + 1) * LANES],
                                 jnp.uint32) for g in range(GROUPS_PER_TILE)]
            per_time = _swap_atoms(src, GROUPS_PER_TILE, GROUPS_PER_TILE)
            for j in range(GROUPS_PER_TILE):
                t = k * GROUPS_PER_TILE + j
                u_t = u_refs[t][pl.ds(r0, ROWS_PER_PACKED_VREG), :].astype(F32)
                y_t = pltpu.bitcast(per_time[j], BF16).astype(F32) + d_ref[0] * u_t
                y_ref[t, pl.ds(r0, ROWS_PER_PACKED_VREG), :] = y_t.astype(y_ref.dtype)
        return carry

    lax.fori_loop(0, rows // ROWS_PER_PACKED_VREG, scatter_rows, 0)


def _s5_core(proj, w_op, m_op, v_op, a_op, d_skip, shapes):
    T = S5_CHUNK
    n_tok, pw = proj.shape
    rows = shapes[0][0] * shapes[0][1]
    assert all(b * c == rows for b, c in shapes) and rows * T * len(shapes) == n_tok
    tiles_in = pw // LANES
    width = T * S5_GROUP
    u_view = proj.reshape(n_tok // T, T * pw)

    def in_map(s):
        return lambda r, o: (r, s * tiles_in + o)

    op_spec = pl.BlockSpec((GROUPS_PER_TILE, width, width), lambda r, o: (o, 0, 0))
    return pl.pallas_call(
        functools.partial(_s5_core_kernel, shapes=tuple(shapes)),
        grid=(len(shapes), S5_GROUPS // GROUPS_PER_TILE),
        in_specs=[pl.BlockSpec((rows, LANES), in_map(s)) for s in range(T)]
        + [op_spec, op_spec, op_spec, pl.BlockSpec((1, 2, GROUPS_PER_TILE, LANES), lambda r, o: (o, 0, 0, 0)),
           pl.BlockSpec((1, 1, LANES), lambda r, o: (o, 0, 0))],
        out_specs=pl.BlockSpec((T, rows, LANES), lambda r, o: (0, r, o)),
        out_shape=jax.ShapeDtypeStruct((T, n_tok // T, S5_WIDTH), BF16),
        scratch_shapes=[pltpu.VMEM((rows, GROUPS_PER_TILE * width), BF16),
                        pltpu.VMEM((rows, GROUPS_PER_TILE * width), BF16),
                        ] + [pltpu.VMEM((GROUPS_PER_TILE * rows, LANES), F32)] * 4,
        compiler_params=_cparams(("parallel", "parallel")),
    )(*([u_view] * T), w_op, m_op, v_op,
      a_op.reshape(S5_GROUPS // GROUPS_PER_TILE, GROUPS_PER_TILE, 2, LANES).transpose(0, 2, 1, 3),
      d_skip.reshape(S5_WIDTH // LANES, 1, LANES))


def _s5_post_kernel(y_ref, wg_ref, bg_ref, z_ref):
    y = y_ref[0].astype(F32)
    act = 0.5 * y * (1.0 + jnp.tanh(math.sqrt(2.0 / math.pi) * (y + 0.044715 * (y * y * y))))
    gl = _dot(act.astype(BF16), wg_ref[...]) + bg_ref[...]
    z_ref[...] = (act * _sigmoid(gl)).astype(z_ref.dtype)


def _s5_post(y_planes, wg, bg, bm):
    planes, rows, w = y_planes.shape
    z = pl.pallas_call(
        _s5_post_kernel,
        grid=(rows // bm, planes),
        in_specs=[pl.BlockSpec((1, bm, w), lambda i, t: (t, i, 0)),
                  pl.BlockSpec((w, w), lambda i, t: (0, 0)),
                  pl.BlockSpec((1, w), lambda i, t: (0, 0))],
        out_specs=pl.BlockSpec((bm, w), lambda i, t: (i, t)),
        out_shape=jax.ShapeDtypeStruct((rows, planes * w), BF16),
        compiler_params=_cparams(("parallel", "parallel")),
    )(y_planes, wg, bg.reshape(1, w))
    return z.reshape(rows * planes, w)


def _rms_norm(x, g):
    return x * lax.rsqrt(jnp.mean(x * x, axis=-1, keepdims=True) + RMS_EPS) * g


def _rope_block(r, cos, sin):
    return r * cos + pltpu.roll(r, LANES // 2, 1) * sin


def _qproj_kernel(cq_ref, g_ref, w_ref, cos_ref, sin_ref, q_ref, *, heads, scale):
    xn = _rms_norm(cq_ref[...].astype(F32), g_ref[...]).astype(BF16)
    acc = _dot(xn, w_ref[...])
    cos, sin = cos_ref[...], sin_ref[...]
    for h in range(heads):
        lo = h * HEAD_PAD
        q_ref[:, lo:lo + LANES] = (acc[:, lo:lo + LANES] * scale).astype(q_ref.dtype)
        rope = _rope_block(acc[:, lo + LANES:lo + HEAD_PAD], cos, sin)
        q_ref[:, lo + LANES:lo + HEAD_PAD] = (rope * scale).astype(q_ref.dtype)


def _qproj(proj, g, w, cos, sin, bm, pos_block, heads_per_step=4):
    n = proj.shape[0]
    nq = w.shape[1]
    bn = heads_per_step * HEAD_PAD
    scale = (QK_NOPE + QK_ROPE) ** -0.5 * math.log2(math.e)
    return pl.pallas_call(
        functools.partial(_qproj_kernel, heads=heads_per_step, scale=scale),
        grid=(n // bm, nq // bn),
        in_specs=[pl.BlockSpec((bm, Q_LORA), lambda i, j: (i, COL_CQ // Q_LORA)),
                  pl.BlockSpec((1, Q_LORA), lambda i, j: (0, 0)),
                  pl.BlockSpec((Q_LORA, bn), lambda i, j: (0, j)),
                  pl.BlockSpec((bm, LANES), lambda i, j: (pos_block(i), 0)),
                  pl.BlockSpec((bm, LANES), lambda i, j: (pos_block(i), 0))],
        out_specs=pl.BlockSpec((bm, bn), lambda i, j: (i, j)),
        out_shape=jax.ShapeDtypeStruct((n, nq), BF16),
        compiler_params=_cparams(("parallel", "parallel")),
    )(proj, g.reshape(1, Q_LORA), w, cos, sin)


def _kvproj_kernel(ckv_ref, kr_ref, g_ref, wk_ref, wv_ref, cos_ref, sin_ref, k_ref, v_ref):
    xn = _rms_norm(ckv_ref[...].astype(F32), g_ref[...]).astype(BF16)
    kn = _dot(xn, wk_ref[...])
    vv = _dot(xn, wv_ref[...])
    kro = _rope_block(kr_ref[...].astype(F32), cos_ref[...], sin_ref[...]).astype(k_ref.dtype)
    ones = (lax.broadcasted_iota(jnp.int32, kro.shape, 1) == 0).astype(v_ref.dtype)
    for h in range(MLA_HEADS):
        lo = h * HEAD_PAD
        k_ref[:, lo:lo + LANES] = kn[:, h * QK_NOPE:(h + 1) * QK_NOPE].astype(k_ref.dtype)
        k_ref[:, lo + LANES:lo + HEAD_PAD] = kro
        v_ref[:, 2 * h * V_HEAD:(2 * h + 1) * V_HEAD] = vv[:, h * V_HEAD:(h + 1) * V_HEAD].astype(v_ref.dtype)
        v_ref[:, (2 * h + 1) * V_HEAD:(2 * h + 2) * V_HEAD] = ones


def _kvproj(proj, g, wk, wv, cos, sin, bm, pos_block):
    n = proj.shape[0]
    return pl.pallas_call(
        _kvproj_kernel,
        grid=(n // bm,),
        in_specs=[pl.BlockSpec((bm, KV_LORA), lambda i: (i, COL_CKV // KV_LORA)),
                  pl.BlockSpec((bm, LANES), lambda i: (i, COL_KR // LANES)),
                  pl.BlockSpec((1, KV_LORA), lambda i: (0, 0)),
                  pl.BlockSpec(wk.shape, lambda i: (0, 0)),
                  pl.BlockSpec(wv.shape, lambda i: (0, 0)),
                  pl.BlockSpec((bm, LANES), lambda i: (pos_block(i), 0)),
                  pl.BlockSpec((bm, LANES), lambda i: (pos_block(i), 0))],
        out_specs=[pl.BlockSpec((bm, MLA_HEADS * HEAD_PAD), lambda i: (i, 0)),
                   pl.BlockSpec((bm, MLA_HEADS * 2 * V_HEAD), lambda i: (i, 0))],
        out_shape=[jax.ShapeDtypeStruct((n, MLA_HEADS * HEAD_PAD), BF16),
                   jax.ShapeDtypeStruct((n, MLA_HEADS * 2 * V_HEAD), BF16)],
        compiler_params=_cparams(("parallel",)),
    )(proj, proj, g.reshape(1, KV_LORA), wk, wv, cos, sin)


def _attn_kernel(q_ref, k_ref, v_ref, o_ref, *, n_chunks, tk):
    q = q_ref[...]

    def scores(c):
        return lax.dot_general(q, k_ref[c * tk:(c + 1) * tk, :], (((1,), (1,)), ((), ())),
                               preferred_element_type=F32)

    m = jnp.full((q.shape[0], 1), -jnp.inf, F32)
    acc = jnp.zeros((q.shape[0], 2 * V_HEAD), F32)
    s = scores(0)
    for c in range(n_chunks):
        s_next = scores(c + 1) if c + 1 < n_chunks else None
        m_new = jnp.maximum(m, jnp.max(s, axis=-1, keepdims=True))
        p = jnp.exp2((s - m_new).astype(BF16))
        acc = jnp.exp2(m - m_new) * acc + _dot(p, v_ref[c * tk:(c + 1) * tk, :])
        m, s = m_new, s_next
    o_ref[...] = (acc[:, :V_HEAD] / acc[:, V_HEAD:V_HEAD + 1]).astype(o_ref.dtype)


def _attention(q, k, v, base, batch, length, tq):
    nq = length // tq
    row0 = base // tq
    seq0 = base // length
    tk = min(1024, length)
    return pl.pallas_call(
        functools.partial(_attn_kernel, n_chunks=length // tk, tk=tk),
        grid=(batch, MLA_HEADS, nq),
        in_specs=[pl.BlockSpec((tq, HEAD_PAD), lambda b, h, i: (row0 + b * nq + i, h)),
                  pl.BlockSpec((length, HEAD_PAD), lambda b, h, i: (seq0 + b, h)),
                  pl.BlockSpec((length, 2 * V_HEAD), lambda b, h, i: (seq0 + b, h))],
        out_specs=pl.BlockSpec((tq, V_HEAD), lambda b, h, i: (b * nq + i, h)),
        out_shape=jax.ShapeDtypeStruct((batch * length, MLA_HEADS * V_HEAD), BF16),
        compiler_params=_cparams(("parallel", "parallel", "arbitrary")),
    )(q, k, v)


def _merge_kernel(z_ref, o1_ref, o2_ref, gs_ref, gm_ref, wsp_ref, wmp_ref, out_ref, *, split):
    ys = _dot(z_ref[...], wsp_ref[...])
    o = jnp.where(pl.program_id(0) < split, o1_ref[...], o2_ref[...])
    ym = _dot(o, wmp_ref[...])
    out = _sigmoid(gs_ref[...].astype(F32)) * ys + _sigmoid(gm_ref[...].astype(F32)) * ym
    out_ref[...] = out.astype(out_ref.dtype)


def _merge(z, o1, o2, proj, wsp, wmp, bm):
    n = z.shape[0]
    d = D_MODEL
    split = o1.shape[0] // bm
    return pl.pallas_call(
        functools.partial(_merge_kernel, split=split),
        grid=(n // bm,),
        in_specs=[pl.BlockSpec((bm, S5_WIDTH), lambda i: (i, 0)),
                  pl.BlockSpec((bm, d), lambda i: (jnp.minimum(i, split - 1), 0)),
                  pl.BlockSpec((bm, d), lambda i: (jnp.maximum(i - split, 0), 0)),
                  pl.BlockSpec((bm, d), lambda i: (i, COL_GS // d)),
                  pl.BlockSpec((bm, d), lambda i: (i, COL_GM // d)),
                  pl.BlockSpec(wsp.shape, lambda i: (0, 0)),
                  pl.BlockSpec(wmp.shape, lambda i: (0, 0))],
        out_specs=pl.BlockSpec((bm, d), lambda i: (i, 0)),
        out_shape=jax.ShapeDtypeStruct((n, d), BF16),
        compiler_params=_cparams(("parallel",)),
    )(z, o1, o2, proj, proj, wsp, wmp)


def _out_kernel(mg_ref, x_ref, mod_ref, wo_ref, lng_ref, lnb_ref, wr_ref, xo_ref, h2_ref, aff_ref):
    m = mod_ref[0]
    out = _dot(mg_ref[...], wo_ref[...])
    xn = _layer_norm(DEEPNORM_ALPHA * x_ref[...] + m[2:3] * out, lng_ref[...], lnb_ref[...])
    xo_ref[...] = xn
    h2 = (xn * (1.0 + m[4:5]) + m[3:4]).astype(BF16)
    h2_ref[...] = h2
    logits = _dot(h2, wr_ref[...])
    lane = lax.broadcasted_iota(jnp.int32, logits.shape, 1)
    logits = jnp.where(lane < N_EXPERTS, logits, -jnp.inf)
    e = jnp.exp(logits - jnp.max(logits, axis=-1, keepdims=True))
    aff_ref[...] = e / jnp.sum(e, axis=-1, keepdims=True)


def _out_proj(merged, x, mod, wo, lng, lnb, wr, bm, seg):
    n, d = x.shape
    return pl.pallas_call(
        _out_kernel,
        grid=(n // bm,),
        in_specs=[pl.BlockSpec((bm, d), lambda i: (i, 0)),
                  pl.BlockSpec((bm, d), lambda i: (i, 0)),
                  pl.BlockSpec((1, 6, d), lambda i: ((i * bm) // seg, 0, 0)),
                  pl.BlockSpec((d, d), lambda i: (0, 0)),
                  pl.BlockSpec((1, d), lambda i: (0, 0)),
                  pl.BlockSpec((1, d), lambda i: (0, 0)),
                  pl.BlockSpec((d, LANES), lambda i: (0, 0))],
        out_specs=[pl.BlockSpec((bm, d), lambda i: (i, 0)),
                   pl.BlockSpec((bm, d), lambda i: (i, 0)),
                   pl.BlockSpec((bm, LANES), lambda i: (i, 0))],
        out_shape=[jax.ShapeDtypeStruct((n, d), F32),
                   jax.ShapeDtypeStruct((n, d), BF16),
                   jax.ShapeDtypeStruct((n, LANES), F32)],
        compiler_params=_cparams(("parallel",)),
    )(merged, x, mod, wo, lng.reshape(1, d), lnb.reshape(1, d), wr)


def _select_kernel(aff_ref, gate_ref, cnt_ref, off_ref, *, n, cap, bm):
    aff = aff_ref[...]
    bits = pltpu.bitcast(aff, jnp.int32)
    idx = lax.broadcasted_iota(jnp.int32, aff.shape, 0)
    valid = lax.broadcasted_iota(jnp.int32, (1, LANES), 1) < N_EXPERTS

    def count(mask):
        return jnp.sum(mask.astype(jnp.int32), axis=0, keepdims=True)

    def thr_step(k, thr):
        cand = thr | jnp.left_shift(jnp.int32(1), 30 - k)
        return jnp.where(count(bits >= cand) >= cap, cand, thr)

    thr = lax.fori_loop(0, 31, thr_step, jnp.zeros((1, LANES), jnp.int32))
    above = bits > thr
    tied = bits == thr
    need = cap - count(above)
    idx_bits = max(1, (n - 1).bit_length())

    def idx_step(k, bound):
        cand = bound | jnp.left_shift(jnp.int32(1), idx_bits - 1 - k)
        return jnp.where(count(tied & (idx < cand)) < need, cand, bound)

    bound = lax.fori_loop(0, idx_bits, idx_step, jnp.zeros((1, LANES), jnp.int32))
    sel = (above | (tied & (idx <= bound))) & valid
    gate_ref[...] = jnp.where(sel, aff, -1.0)
    nt = n // bm
    cnt = jnp.sum(sel.astype(F32).reshape(nt, bm, LANES), axis=1)
    r = lax.broadcasted_iota(jnp.int32, (nt, nt), 0)
    c = lax.broadcasted_iota(jnp.int32, (nt, nt), 1)
    off = _dot((c < r).astype(BF16), cnt.astype(BF16))
    cnt_ref[0] = cnt.astype(jnp.int32)
    off_ref[0] = off.astype(jnp.int32)


def _select(aff, n_sets, n, cap, bm):
    nt = n // bm
    return pl.pallas_call(
        functools.partial(_select_kernel, n=n, cap=cap, bm=bm),
        grid=(n_sets,),
        in_specs=[pl.BlockSpec((n, LANES), lambda g: (g, 0))],
        out_specs=[pl.BlockSpec((n, LANES), lambda g: (g, 0)),
                   pl.BlockSpec((1, nt, LANES), lambda g: (g, 0, 0)),
                   pl.BlockSpec((1, nt, LANES), lambda g: (g, 0, 0))],
        out_shape=[jax.ShapeDtypeStruct((n_sets * n, LANES), F32),
                   jax.ShapeDtypeStruct((n_sets, nt, LANES), jnp.int32),
                   jax.ShapeDtypeStruct((n_sets, nt, LANES), jnp.int32)],
        compiler_params=_cparams(("parallel",)),
    )(aff)


def _replication_matrices():
    block = jnp.arange(N_EXPERTS * MOE_SLAB) // MOE_SLAB
    r1 = (jnp.arange(LANES)[:, None] == block[None, :]).astype(BF16)
    return r1, jnp.concatenate([SLOT_ALIGN * r1, r1], axis=0)


def _slot_onehot(gate, pos, offs, n_pass, shifts, r1_ref, r2_ref, weighted):
    lane = lax.broadcasted_iota(jnp.int32, (1, LANES), 1)
    col = lax.broadcasted_iota(jnp.int32, (1, N_EXPERTS * MOE_SLAB), 1)
    rem = jnp.zeros((1, LANES), F32)
    for e in range(N_EXPERTS):
        rem = jnp.where(lane == e, (offs[e] - _align_down(offs[e])).astype(F32), rem)
    rel = pos + rem
    hi = jnp.floor(rel * (1.0 / SLOT_ALIGN))
    digits = jnp.concatenate([hi, rel - SLOT_ALIGN * hi], axis=1).astype(BF16)
    rel_rep = _dot(digits, r2_ref[...])
    gate_rep = _dot(gate.astype(BF16), r1_ref[...])
    base = n_pass * MOE_SLAB
    base = float(base) if isinstance(base, int) else base.astype(F32)
    row = (col % MOE_SLAB).astype(F32)
    if shifts is None:
        target = row + base
    else:
        shift_rep = jnp.zeros(row.shape, F32)
        for e in range(N_EXPERTS):
            shift_rep = jnp.where(col // MOE_SLAB == e, shifts[e].astype(F32), shift_rep)
        target = jnp.where(row >= shift_rep, row - shift_rep + base, -1.0)
    hit = (gate_rep >= 0.0) & (rel_rep == target)
    return jnp.where(hit, gate_rep if weighted else 1.0, 0.0).astype(BF16)


def _slot_positions(gate):
    bm = gate.shape[0]
    r = lax.broadcasted_iota(jnp.int32, (bm, bm), 0)
    c = lax.broadcasted_iota(jnp.int32, (bm, bm), 1)
    return _dot((c < r).astype(BF16), (gate >= 0.0).astype(BF16))


def _align_down(v):
    return (v // SLOT_ALIGN) * SLOT_ALIGN


def _aligned(v):
    return v if isinstance(v, int) else pl.multiple_of(v, SLOT_ALIGN)


def _compact_kernel(off_ref, npass_ref, coff_ref, nsub_ref, h_ref, gate_ref, r1_ref, r2_ref, xe_hbm,
                    slab_ref, carry_ref, sem, pending_ref, *, nt, cap, n_steps):
    g = pl.program_id(0)
    i = pl.program_id(1)
    step = g * nt + i
    buf = step % 2
    base = step * N_EXPERTS
    offs = [off_ref[base + e] for e in range(N_EXPERTS)]
    npass = [npass_ref[base + e] for e in range(N_EXPERTS)]

    def window_copy(which, e, lo):
        return pltpu.make_async_copy(
            slab_ref.at[which, pl.ds(e * MOE_SLAB, MOE_SLAB)],
            xe_hbm.at[g, e, pl.ds(_aligned(lo), MOE_SLAB)], sem.at[which, e])

    def drain(which):
        for e in range(N_EXPERTS):
            window_copy(which, e, 0).wait()

    @pl.when(step == 0)
    def _():
        pending_ref[0] = 0

    @pl.when(i == 0)
    def _():
        carry_ref[...] = jnp.zeros_like(carry_ref)
        slab_ref[buf] = jnp.zeros(slab_ref.shape[1:], slab_ref.dtype)
        for e in range(N_EXPERTS):
            window_copy(buf, e, cap).start()
        drain(buf)

    gate = gate_ref[...]
    pos = _slot_positions(gate)

    def build(s):
        lo = [_align_down(offs[e]) + s * MOE_SLAB for e in range(N_EXPERTS)]
        sel_t = _slot_onehot(gate, pos, offs, s, None, r1_ref, r2_ref, weighted=False)
        slab = lax.dot_general(sel_t, h_ref[...], (((0,), (0,)), ((), ())), preferred_element_type=F32)
        slab_ref[buf] = slab.astype(slab_ref.dtype)
        return lo

    def keep_tail(s):
        for e in range(N_EXPERTS):
            @pl.when(s == npass[e] - 1)
            def _(e=e):
                src = pl.multiple_of(e * MOE_SLAB + coff_ref[base + e], SLOT_ALIGN)
                carry_ref[pl.ds(e * SLOT_ALIGN, SLOT_ALIGN), :] = slab_ref[buf, pl.ds(src, SLOT_ALIGN), :]

    lo0 = build(0)
    for e in range(N_EXPERTS):
        rows = pl.ds(e * MOE_SLAB, SLOT_ALIGN)
        merged = slab_ref[buf, rows, :].astype(F32) + carry_ref[pl.ds(e * SLOT_ALIGN, SLOT_ALIGN), :].astype(F32)
        slab_ref[buf, rows, :] = merged.astype(slab_ref.dtype)
    keep_tail(0)

    @pl.when(pending_ref[0] == 1)
    def _():
        drain(1 - buf)

    for e in range(N_EXPERTS):
        window_copy(buf, e, lo0[e]).start()
    pending_ref[0] = 1

    n_pass = nsub_ref[step]

    @pl.when((n_pass > 1) | (step == n_steps - 1))
    def _():
        drain(buf)
        pending_ref[0] = 0

    def extra_pass(s, carry):
        lo = build(s)
        keep_tail(s)
        for e in range(N_EXPERTS):
            @pl.when(s < npass[e])
            def _(e=e):
                window_copy(buf, e, lo[e]).start()
        for e in range(N_EXPERTS):
            @pl.when(s < npass[e])
            def _(e=e):
                window_copy(buf, e, lo[e]).wait()
        return carry

    lax.fori_loop(1, n_pass, extra_pass, 0)


def _compact(off, npass, coff, nsub, h2, gate, reps, n_sets, n, cap, bm):
    nt = n // bm
    d = h2.shape[1]
    grid_spec = pltpu.PrefetchScalarGridSpec(
        num_scalar_prefetch=4,
        grid=(n_sets, nt),
        in_specs=[pl.BlockSpec((bm, d), lambda g, i, *_: (g * nt + i, 0)),
                  pl.BlockSpec((bm, LANES), lambda g, i, *_: (g * nt + i, 0)),
                  pl.BlockSpec(reps[0].shape, lambda g, i, *_: (0, 0)),
                  pl.BlockSpec(reps[1].shape, lambda g, i, *_: (0, 0))],
        out_specs=pl.BlockSpec(memory_space=pl.ANY),
        scratch_shapes=[pltpu.VMEM((2, N_EXPERTS * MOE_SLAB, d), BF16),
                        pltpu.VMEM((N_EXPERTS * SLOT_ALIGN, d), BF16),
                        pltpu.SemaphoreType.DMA((2, N_EXPERTS)),
                        pltpu.SMEM((1,), jnp.int32)],
    )
    return pl.pallas_call(
        functools.partial(_compact_kernel, nt=nt, cap=cap, n_steps=n_sets * nt),
        grid_spec=grid_spec,
        out_shape=jax.ShapeDtypeStruct((n_sets, N_EXPERTS, cap + MOE_SLAB, d), BF16),
        compiler_params=_cparams(("arbitrary", "arbitrary")),
    )(off, npass, coff, nsub, h2, gate, *reps)


def _ffn_kernel(xe_ref, w1_ref, w3_ref, w2_ref, ye_ref):
    x = xe_ref[0, 0]
    a = _dot(x, w1_ref[0])
    b = _dot(x, w3_ref[0])
    hid = (a * _sigmoid(a) * b).astype(BF16)
    ye_ref[0, 0] = _dot(hid, w2_ref[0]).astype(ye_ref.dtype)


def _ffn(xe, w1, w3, w2, n_sets, cap, tm):
    d = xe.shape[-1]
    f = w1.shape[-1]
    return pl.pallas_call(
        _ffn_kernel,
        grid=(N_EXPERTS, n_sets, cap // tm),
        in_specs=[pl.BlockSpec((1, 1, tm, d), lambda e, g, t: (g, e, t, 0)),
                  pl.BlockSpec((1, d, f), lambda e, g, t: (e, 0, 0)),
                  pl.BlockSpec((1, d, f), lambda e, g, t: (e, 0, 0)),
                  pl.BlockSpec((1, f, d), lambda e, g, t: (e, 0, 0))],
        out_specs=pl.BlockSpec((1, 1, tm, d), lambda e, g, t: (g, e, t, 0)),
        out_shape=jax.ShapeDtypeStruct((n_sets, N_EXPERTS, cap, d), BF16),
        compiler_params=_cparams(("parallel", "parallel", "parallel")),
    )(xe, w1, w3, w2)


def _combine_kernel(off_ref, nsub_ref, gate_ref, x_ref, mod_ref, lng_ref, lnb_ref, r1_ref, r2_ref, ye_hbm,
                    xo_ref, slab_ref, sem, *, nt, cap, n_steps):
    g = pl.program_id(0)
    i = pl.program_id(1)
    step = g * nt + i
    buf = step % 2
    gate = gate_ref[...]
    pos = _slot_positions(gate)

    def windows(at_step, s):
        offs = [off_ref[at_step * N_EXPERTS + e] for e in range(N_EXPERTS)]
        lo = [_align_down(offs[e]) + s * MOE_SLAB for e in range(N_EXPERTS)]
        start = [jnp.minimum(lo[e], cap - MOE_SLAB) for e in range(N_EXPERTS)]
        return offs, lo, start

    def fetch(which, at_step, s):
        _, _, start = windows(at_step, s)
        set_idx = at_step // nt
        return [pltpu.make_async_copy(
            ye_hbm.at[set_idx, e, pl.ds(_aligned(start[e]), MOE_SLAB)],
            slab_ref.at[which, pl.ds(e * MOE_SLAB, MOE_SLAB)], sem.at[which, e]) for e in range(N_EXPERTS)]

    def spread_dot(which, s):
        offs, lo, start = windows(step, s)
        shifts = [lo[e] - start[e] for e in range(N_EXPERTS)]
        spread = _slot_onehot(gate, pos, offs, s, shifts, r1_ref, r2_ref, weighted=True)
        return _dot(spread, slab_ref[which])

    @pl.when(step == 0)
    def _():
        for cp in fetch(buf, step, 0):
            cp.start()

    @pl.when(step + 1 < n_steps)
    def _():
        for cp in fetch(1 - buf, step + 1, 0):
            cp.start()

    for cp in fetch(buf, step, 0):
        cp.wait()
    moe = spread_dot(buf, 0)

    def extra_pass(s, acc):
        copies = fetch(buf, step, s)
        for cp in copies:
            cp.start()
        for cp in copies:
            cp.wait()
        return acc + spread_dot(buf, s)

    moe = lax.fori_loop(1, nsub_ref[step], extra_pass, moe)
    m = mod_ref[0]
    xo_ref[...] = _layer_norm(DEEPNORM_ALPHA * x_ref[...] + m[5:6] * moe, lng_ref[...], lnb_ref[...])


def _combine(off, nsub, gate, x, mod, lng, lnb, reps, ye, n_sets, n, cap, bm, seg):
    nt = n // bm
    d = x.shape[1]
    grid_spec = pltpu.PrefetchScalarGridSpec(
        num_scalar_prefetch=2,
        grid=(n_sets, nt),
        in_specs=[pl.BlockSpec((bm, LANES), lambda g, i, *_: (g * nt + i, 0)),
                  pl.BlockSpec((bm, d), lambda g, i, *_: (g * nt + i, 0)),
                  pl.BlockSpec((1, 6, d), lambda g, i, *_: (((g * nt + i) * bm) // seg, 0, 0)),
                  pl.BlockSpec((1, d), lambda g, i, *_: (0, 0)),
                  pl.BlockSpec((1, d), lambda g, i, *_: (0, 0)),
                  pl.BlockSpec(reps[0].shape, lambda g, i, *_: (0, 0)),
                  pl.BlockSpec(reps[1].shape, lambda g, i, *_: (0, 0)),
                  pl.BlockSpec(memory_space=pl.ANY)],
        out_specs=pl.BlockSpec((bm, d), lambda g, i, *_: (g * nt + i, 0)),
        scratch_shapes=[pltpu.VMEM((2, N_EXPERTS * MOE_SLAB, d), BF16),
                        pltpu.SemaphoreType.DMA((2, N_EXPERTS))],
    )
    return pl.pallas_call(
        functools.partial(_combine_kernel, nt=nt, cap=cap, n_steps=n_sets * nt),
        grid_spec=grid_spec,
        out_shape=jax.ShapeDtypeStruct(x.shape, F32),
        compiler_params=_cparams(("arbitrary", "arbitrary")),
    )(off, nsub, gate, x, mod, lng.reshape(1, d), lnb.reshape(1, d), *reps, ye)


def _rope_lanes(w):
    z = jnp.zeros(w.shape[:-1] + (ROPE_HALF,), w.dtype)
    return jnp.concatenate([w[..., :ROPE_HALF], z, w[..., ROPE_HALF:], z], axis=-1)


def _proj_weight(w_in):
    c0 = S5_WIDTH + Q_LORA + KV_LORA
    kr = w_in[:, c0:c0 + QK_ROPE]
    gates = w_in[:, c0 + QK_ROPE:]
    return jnp.concatenate([w_in[:, :c0], gates, _rope_lanes(kr)], axis=1).astype(BF16)


def _q_weight(w_uq):
    w = w_uq.reshape(Q_LORA, MLA_HEADS, QK_NOPE + QK_ROPE)
    w = jnp.concatenate([w[..., :QK_NOPE], _rope_lanes(w[..., QK_NOPE:])], axis=-1)
    return w.reshape(Q_LORA, MLA_HEADS * HEAD_PAD).astype(BF16)


def _kv_weights(w_ukv):
    w = w_ukv.reshape(KV_LORA, MLA_HEADS, QK_NOPE + V_HEAD)
    wk = w[..., :QK_NOPE].reshape(KV_LORA, MLA_HEADS * QK_NOPE)
    wv = w[..., QK_NOPE:].reshape(KV_LORA, MLA_HEADS * V_HEAD)
    return wk.astype(BF16), wv.astype(BF16)


def _rope_tables(length):
    pos = jnp.arange(length, dtype=F32)
    inv = 1.0 / (ROPE_BASE ** (jnp.arange(0, QK_ROPE, 2, dtype=F32) / QK_ROPE))
    ang = pos[:, None] * inv[None, :]
    cos, sin = jnp.cos(ang), jnp.sin(ang)
    z = jnp.zeros_like(cos)
    return (jnp.concatenate([cos, z, cos, z], axis=1), jnp.concatenate([-sin, z, sin, z], axis=1))


def _trunk(x_groups, c_groups, p):
    (b1, l1, d), (b2, l2, _) = x_groups[0].shape, x_groups[1].shape
    n1, n2 = b1 * l1, b2 * l2
    n = n1 + n2
    seg = l1
    assert l2 % l1 == 0 and n1 == n2 and n1 % l2 == 0
    cap = EC_CAPACITY * n1 // N_EXPERTS
    bm = min(512, seg)
    bm_in = min(1024, seg)
    bm_moe = min(256, seg)
    tq = min(1024, l1)
    tm = min(256, cap)
    assert cap >= MOE_SLAB and cap % tm == 0

    x = jnp.concatenate([x_groups[0].reshape(n1, d), x_groups[1].reshape(n2, d)], axis=0)
    x = _ln_in(x, p['ln_in_g'], p['ln_in_b'], bm)

    c = jnp.concatenate(list(c_groups), axis=0)
    c_rows = -(-c.shape[0] // SUBLANES) * SUBLANES
    mod_all = _ada(jnp.pad(c, ((0, c_rows - c.shape[0]), (0, 0))), p['w_ada'], p['b_ada'])
    seg_rows = jnp.array(list(range(b1)) + [b1 + b for b in range(b2) for _ in range(l2 // seg)])
    mod_all = mod_all.reshape(DEPTH, c_rows, 6, d)[:, seg_rows]

    cos, sin = _rope_tables(max(l1, l2))
    reps = _replication_matrices()

    def pos_block(bm_):
        def f(i):
            return jnp.where(i < n1 // bm_, i % (l1 // bm_), (i - n1 // bm_) % (l2 // bm_))
        return f

    for l in range(DEPTH):
        mod = mod_all[l]
        proj = _inproj(x, mod, _proj_weight(p['w_in'][l]), bm_in, PROJ_W // 7, seg)

        m_op, w_op, v_op, a_op = _s5_operators(p['s5_lam_re'][l], p['s5_lam_im'][l], p['s5_log_step'][l],
                                               p['s5_b_re'][l], p['s5_b_im'][l], p['s5_c_re'][l], p['s5_c_im'][l])
        y_s5 = _s5_core(proj, w_op, m_op, v_op, a_op, p['s5_d'][l],
                        ((b1, l1 // S5_CHUNK), (b2, l2 // S5_CHUNK)))
        z = _s5_post(y_s5, p['s5_w_glu'][l].astype(BF16), p['s5_b_glu'][l], min(bm, n // S5_CHUNK))

        q = _qproj(proj, p['mla_q_norm'][l], _q_weight(p['mla_w_uq'][l]), cos, sin, bm, pos_block(bm))
        wk, wv = _kv_weights(p['mla_w_ukv'][l])
        k, v = _kvproj(proj, p['mla_kv_norm'][l], wk, wv, cos, sin, bm, pos_block(bm))
        o1 = _attention(q, k, v, 0, b1, l1, tq)
        o2 = _attention(q, k, v, n1, b2, l2, tq)

        merged = _merge(z, o1, o2, proj, p['w_s5_proj'][l].astype(BF16), p['w_mla_proj'][l].astype(BF16), bm)
        wr = jnp.pad(p['w_router'][l], ((0, 0), (0, LANES - N_EXPERTS))).astype(BF16)
        x, h2, aff = _out_proj(merged, x, mod, p['w_o'][l].astype(BF16), p['ln1_g'][l], p['ln1_b'][l],
                               wr, bm, seg)

        gate, cnt, off = _select(aff, 2, n1, cap, bm_moe)
        cnt, off = cnt[..., :N_EXPERTS], off[..., :N_EXPERTS]
        filled = off % SLOT_ALIGN + cnt
        npass = filled // MOE_SLAB + 1
        coff = (filled // SLOT_ALIGN) * SLOT_ALIGN - (npass - 1) * MOE_SLAB
        nsub = jnp.max(npass, axis=-1).reshape(-1)
        off, npass, coff = off.reshape(-1), npass.reshape(-1), coff.reshape(-1)
        xe = _compact(off, npass, coff, nsub, h2, gate, reps, 2, n1, cap, bm_moe)
        ye = _ffn(xe, p['w_exp1'][l].astype(BF16), p['w_exp3'][l].astype(BF16),
                  p['w_exp2'][l].astype(BF16), 2, cap, tm)
        x = _combine(off, nsub, gate, x, mod, p['ln2_g'][l], p['ln2_b'][l], reps, ye, 2, n1, cap, bm_moe, seg)

    return x[:n1].reshape(b1, l1, d), x[n1:].reshape(b2, l2, d)


def kernel(x_prompt, x_sample, c_prompt, c_sample, ln_in_g, ln_in_b, w_ada, b_ada, w_in, s5_lam_re, s5_lam_im, s5_log_step, s5_b_re, s5_b_im, s5_c_re, s5_c_im, s5_d, s5_w_glu, s5_b_glu, w_s5_proj, mla_q_norm, mla_w_uq, mla_kv_norm, mla_w_ukv, w_mla_proj, w_o, ln1_g, ln1_b, w_router, w_exp1, w_exp3, w_exp2, ln2_g, ln2_b):
    params = dict(ln_in_g=ln_in_g, ln_in_b=ln_in_b, w_ada=w_ada, b_ada=b_ada, w_in=w_in,
                  s5_lam_re=s5_lam_re, s5_lam_im=s5_lam_im, s5_log_step=s5_log_step,
                  s5_b_re=s5_b_re, s5_b_im=s5_b_im, s5_c_re=s5_c_re, s5_c_im=s5_c_im, s5_d=s5_d,
                  s5_w_glu=s5_w_glu, s5_b_glu=s5_b_glu, w_s5_proj=w_s5_proj,
                  mla_q_norm=mla_q_norm, mla_w_uq=mla_w_uq, mla_kv_norm=mla_kv_norm,
                  mla_w_ukv=mla_w_ukv, w_mla_proj=w_mla_proj, w_o=w_o, ln1_g=ln1_g, ln1_b=ln1_b,
                  w_router=w_router, w_exp1=w_exp1, w_exp3=w_exp3, w_exp2=w_exp2,
                  ln2_g=ln2_g, ln2_b=ln2_b)
    return _trunk((x_prompt, x_sample), (c_prompt, c_sample), params)
```

```python
import functools
import math

import jax
import jax.numpy as jnp
from jax import lax
from jax.experimental import pallas as pl
from jax.experimental.pallas import tpu as pltpu

F32 = jnp.float32
BF16 = jnp.bfloat16

D_MODEL = 2048
DEPTH = 4
S5_WIDTH = D_MODEL // 2
S5_GROUP = 16
S5_GROUPS = S5_WIDTH // S5_GROUP
S5_STATE = 64
MLA_HEADS = 16
QK_NOPE = 128
QK_ROPE = 64
V_HEAD = 128
Q_LORA = 512
KV_LORA = 512
ROPE_BASE = 10000.0
N_EXPERTS = 16
EC_CAPACITY = 2
D_EXPERT = 1408
DEEPNORM_ALPHA = (2 * DEPTH) ** 0.25
LN_EPS = 1e-5
RMS_EPS = 1e-6

LANES = 128
SUBLANES = 8
MXU_DIM = 256
S5_CHUNK = MXU_DIM // S5_GROUP
HEAD_PAD = 2 * LANES
ROPE_HALF = QK_ROPE // 2
COL_U, COL_CQ, COL_CKV = 0, S5_WIDTH, S5_WIDTH + Q_LORA
COL_GS = S5_WIDTH + Q_LORA + KV_LORA
COL_GM = COL_GS + D_MODEL
COL_KR = COL_GM + D_MODEL
PROJ_W = COL_KR + LANES
MOE_SLAB = 64
SLOT_ALIGN = 16
VMEM_LIMIT = 56 * 1024 * 1024


def _cparams(sem, vmem=VMEM_LIMIT):
    return pltpu.CompilerParams(dimension_semantics=sem, vmem_limit_bytes=vmem)


def _dot(a, b):
    return jnp.dot(a, b, preferred_element_type=F32)


def _layer_norm(y, g, b):
    mu = jnp.mean(y, axis=-1, keepdims=True)
    yc = y - mu
    var = jnp.mean(yc * yc, axis=-1, keepdims=True)
    return yc * lax.rsqrt(var + LN_EPS) * g + b


def _sigmoid(x):
    return 1.0 / (1.0 + jnp.exp(-x))


def _ln_in_kernel(x_ref, g_ref, b_ref, o_ref):
    o_ref[...] = _layer_norm(x_ref[...], g_ref[...], b_ref[...])


def _ln_in(x, g, b, bm):
    n, d = x.shape
    return pl.pallas_call(
        _ln_in_kernel,
        grid=(n // bm,),
        in_specs=[pl.BlockSpec((bm, d), lambda i: (i, 0)),
                  pl.BlockSpec((1, d), lambda i: (0, 0)),
                  pl.BlockSpec((1, d), lambda i: (0, 0))],
        out_specs=pl.BlockSpec((bm, d), lambda i: (i, 0)),
        out_shape=jax.ShapeDtypeStruct((n, d), F32),
        compiler_params=_cparams(("parallel",)),
    )(x, g.reshape(1, d), b.reshape(1, d))


def _ada_kernel(c_ref, w_ref, b_ref, o_ref):
    c = c_ref[...]
    cond = (c * _sigmoid(c)).astype(BF16)
    o_ref[0] = _dot(cond, w_ref[0].astype(BF16)) + b_ref[0]


def _ada(c_pad, w_ada, b_ada, bn=1024):
    depth, d, n6 = w_ada.shape
    rows = c_pad.shape[0]
    return pl.pallas_call(
        _ada_kernel,
        grid=(depth, n6 // bn),
        in_specs=[pl.BlockSpec((rows, d), lambda l, j: (0, 0)),
                  pl.BlockSpec((1, d, bn), lambda l, j: (l, 0, j)),
                  pl.BlockSpec((1, 1, bn), lambda l, j: (l, 0, j))],
        out_specs=pl.BlockSpec((1, rows, bn), lambda l, j: (l, 0, j)),
        out_shape=jax.ShapeDtypeStruct((depth, rows, n6), F32),
        compiler_params=_cparams(("parallel", "parallel")),
    )(c_pad, w_ada, b_ada.reshape(depth, 1, n6))


def _inproj_kernel(x_ref, mod_ref, w_ref, o_ref, h_ref):
    @pl.when(pl.program_id(1) == 0)
    def _():
        m = mod_ref[0]
        h_ref[...] = (x_ref[...] * (1.0 + m[1:2]) + m[0:1]).astype(BF16)

    o_ref[...] = _dot(h_ref[...], w_ref[...]).astype(o_ref.dtype)


def _inproj(x, mod, w, bm, bn, seg):
    n, d = x.shape
    nw = w.shape[1]
    return pl.pallas_call(
        _inproj_kernel,
        grid=(n // bm, nw // bn),
        in_specs=[pl.BlockSpec((bm, d), lambda i, j: (i, 0)),
                  pl.BlockSpec((1, 6, d), lambda i, j: ((i * bm) // seg, 0, 0)),
                  pl.BlockSpec((d, bn), lambda i, j: (0, j))],
        out_specs=pl.BlockSpec((bm, bn), lambda i, j: (i, j)),
        out_shape=jax.ShapeDtypeStruct((n, nw), BF16),
        scratch_shapes=[pltpu.VMEM((bm, d), BF16)],
        compiler_params=_cparams(("parallel", "arbitrary")),
    )(x, mod, w)


def _s5_operators(lam_re, lam_im, log_step, b_re, b_im, c_re, c_im):
    T = S5_CHUNK
    hi = lax.Precision.HIGHEST
    m_tot, w_cols, v_rows, a_re, a_im = 0.0, [], [], [], []
    for direction in range(2):
        lr = lam_re[direction].astype(F32)
        li = lam_im[direction].astype(F32)
        step = jnp.exp(log_step[direction].astype(F32))[:, None]
        mag = jnp.exp(lr * step)
        ar = mag * jnp.cos(li * step)
        ai = mag * jnp.sin(li * step)
        den = lr * lr + li * li
        fr = ((ar - 1.0) * lr + ai * li) / den
        fi = (ai * lr - (ar - 1.0) * li) / den
        br = b_re[direction].astype(F32)
        bi = b_im[direction].astype(F32)
        bbr = fr[..., None] * br - fi[..., None] * bi
        bbi = fr[..., None] * bi + fi[..., None] * br
        cr = c_re[direction].astype(F32)
        ci = c_im[direction].astype(F32)
        pr, pi = [jnp.ones_like(ar)], [jnp.zeros_like(ar)]
        for _ in range(T):
            pr.append(pr[-1] * ar - pi[-1] * ai)
            pi.append(pr[-2] * ai + pi[-1] * ar)
        pr = jnp.stack(pr)
        pi = jnp.stack(pi)
        abr = pr[:T, :, :, None] * bbr[None] - pi[:T, :, :, None] * bbi[None]
        abi = pr[:T, :, :, None] * bbi[None] + pi[:T, :, :, None] * bbr[None]
        kern = (jnp.einsum('ghp,tgpk->tghk', cr, abr, precision=hi)
                - jnp.einsum('ghp,tgpk->tghk', ci, abi, precision=hi))
        s_idx = jnp.arange(T)[:, None]
        t_idx = jnp.arange(T)[None, :]
        tau = (t_idx - s_idx) if direction == 0 else (s_idx - t_idx)
        ktab = kern[jnp.clip(tau, 0, T - 1)]
        ktab = jnp.where((tau >= 0)[:, :, None, None, None], ktab, 0.0)
        m_tot = m_tot + ktab.transpose(2, 0, 4, 1, 3).reshape(S5_GROUPS, T * S5_GROUP, T * S5_GROUP)
        order = jnp.arange(T - 1, -1, -1) if direction == 0 else jnp.arange(T)
        w_cols.append((abr[order].transpose(1, 0, 3, 2).reshape(S5_GROUPS, T * S5_GROUP, S5_STATE),
                       abi[order].transpose(1, 0, 3, 2).reshape(S5_GROUPS, T * S5_GROUP, S5_STATE)))
        k_idx = (jnp.arange(T) + 1) if direction == 0 else (T - jnp.arange(T))
        qr, qi = pr[k_idx], pi[k_idx]
        v_re = cr[None] * qr[:, :, None, :] - ci[None] * qi[:, :, None, :]
        v_im = -(cr[None] * qi[:, :, None, :] + ci[None] * qr[:, :, None, :])
        v_rows.append((v_re.transpose(1, 3, 0, 2).reshape(S5_GROUPS, S5_STATE, T * S5_GROUP),
                       v_im.transpose(1, 3, 0, 2).reshape(S5_GROUPS, S5_STATE, T * S5_GROUP)))
        a_re.append(pr[T])
        a_im.append(pi[T])
    w_op = jnp.concatenate([w_cols[0][0], w_cols[1][0], w_cols[0][1], w_cols[1][1]], axis=2)
    v_op = jnp.concatenate([v_rows[0][0], v_rows[1][0], v_rows[0][1], v_rows[1][1]], axis=1)
    a_op = jnp.stack([jnp.concatenate(a_re, axis=1), jnp.concatenate(a_im, axis=1)], axis=1)
    return m_tot.astype(BF16), w_op.astype(BF16), v_op.astype(BF16), a_op


GROUPS_PER_TILE = LANES // S5_GROUP
ROWS_PER_PACKED_VREG = 2 * SUBLANES


def _swap_atoms(tiles, n_out, n_in):
    atom = lax.broadcasted_iota(jnp.int32, tiles[0].shape, 1) // S5_GROUP
    out = []
    for o in range(n_out):
        acc = None
        for i in range(n_in):
            shift = ((i - o) % GROUPS_PER_TILE) * S5_GROUP
            piece = tiles[i] if shift == 0 else pltpu.roll(tiles[i], shift, 1)
            acc = piece if acc is None else jnp.where(atom == i, piece, acc)
        out.append(acc)
    return out


def _s5_core_kernel(*refs, shapes):
    T = S5_CHUNK
    u_refs, (w_ref, m_ref, v_ref, a_ref, d_ref) = refs[:T], refs[T:T + 5]
    y_ref, x_ref, yo_ref, s0_ref, s1_ref, h0_ref, h1_ref = refs[T + 5:]
    rows = x_ref.shape[0]
    half = S5_STATE
    width = T * S5_GROUP
    halves = width // LANES

    def gather_rows(i, carry):
        r0 = pl.multiple_of(i * ROWS_PER_PACKED_VREG, ROWS_PER_PACKED_VREG)
        src = [pltpu.bitcast(u_refs[s][pl.ds(r0, ROWS_PER_PACKED_VREG), :], jnp.uint32) for s in range(T)]
        for k in range(halves):
            per_group = _swap_atoms(src[k * GROUPS_PER_TILE:(k + 1) * GROUPS_PER_TILE], GROUPS_PER_TILE, GROUPS_PER_TILE)
            for g in range(GROUPS_PER_TILE):
                lo = g * width + k * LANES
                x_ref[pl.ds(r0, ROWS_PER_PACKED_VREG), lo:lo + LANES] = pltpu.bitcast(per_group[g], BF16)
        return carry

    lax.fori_loop(0, rows // ROWS_PER_PACKED_VREG, gather_rows, 0)

    for g in range(GROUPS_PER_TILE):
        s = _dot(x_ref[:, g * width:(g + 1) * width], w_ref[g])
        s0_ref[pl.ds(g, rows, stride=GROUPS_PER_TILE), :] = s[:, :LANES]
        s1_ref[pl.ds(g, rows, stride=GROUPS_PER_TILE), :] = s[:, LANES:]

    is_fwd = lax.broadcasted_iota(jnp.int32, (SUBLANES, LANES), 1) < half
    a_re, a_im = a_ref[0, 0], a_ref[0, 1]

    def scan(batch, n_chunks):
        def body(i, carry):
            new = []
            for b in range(batch):
                rf = pl.multiple_of((b * n_chunks + i) * SUBLANES, SUBLANES)
                rb = pl.multiple_of((b * n_chunks + n_chunks - 1 - i) * SUBLANES, SUBLANES)
                h0, h1 = carry[b]
                h0_ref[pl.ds(rf, SUBLANES), 0:half] = h0[:, :half]
                h1_ref[pl.ds(rf, SUBLANES), 0:half] = h1[:, :half]
                h0_ref[pl.ds(rb, SUBLANES), half:LANES] = h0[:, half:]
                h1_ref[pl.ds(rb, SUBLANES), half:LANES] = h1[:, half:]
                s0 = jnp.where(is_fwd, s0_ref[pl.ds(rf, SUBLANES), :], s0_ref[pl.ds(rb, SUBLANES), :])
                s1 = jnp.where(is_fwd, s1_ref[pl.ds(rf, SUBLANES), :], s1_ref[pl.ds(rb, SUBLANES), :])
                new.append((a_re * h0 - a_im * h1 + s0, a_re * h1 + a_im * h0 + s1))
            return tuple(new)

        zero = jnp.zeros((SUBLANES, LANES), F32)
        lax.fori_loop(0, n_chunks, body, tuple((zero, zero) for _ in range(batch)))

    for idx, (batch, n_chunks) in enumerate(shapes):
        @pl.when(pl.program_id(0) == idx)
        def _(batch=batch, n_chunks=n_chunks):
            scan(batch, n_chunks)

    for g in range(GROUPS_PER_TILE):
        xg = x_ref[:, g * width:(g + 1) * width]
        hg = jnp.concatenate([h0_ref[pl.ds(g, rows, stride=GROUPS_PER_TILE), :],
                              h1_ref[pl.ds(g, rows, stride=GROUPS_PER_TILE), :]], axis=1).astype(BF16)
        y = _dot(xg, m_ref[g]) + _dot(hg, v_ref[g])
        yo_ref[:, g * width:(g + 1) * width] = y.astype(yo_ref.dtype)

    def scatter_rows(i, carry):
        r0 = pl.multiple_of(i * ROWS_PER_PACKED_VREG, ROWS_PER_PACKED_VREG)
        for k in range(halves):
            src = [pltpu.bitcast(yo_ref[pl.ds(r0, ROWS_PER_PACKED_VREG), g * width + k * LANES:g * width + (k + 1) * LANES],
                                 jnp.uint32) for g in range(GROUPS_PER_TILE)]
            per_time = _swap_atoms(src, GROUPS_PER_TILE, GROUPS_PER_TILE)
            for j in range(GROUPS_PER_TILE):
                t = k * GROUPS_PER_TILE + j
                u_t = u_refs[t][pl.ds(r0, ROWS_PER_PACKED_VREG), :].astype(F32)
                y_t = pltpu.bitcast(per_time[j], BF16).astype(F32) + d_ref[0] * u_t
                y_ref[t, pl.ds(r0, ROWS_PER_PACKED_VREG), :] = y_t.astype(y_ref.dtype)
        return carry

    lax.fori_loop(0, rows // ROWS_PER_PACKED_VREG, scatter_rows, 0)


def _s5_core(proj, w_op, m_op, v_op, a_op, d_skip, shapes):
    T = S5_CHUNK
    n_tok, pw = proj.shape
    rows = shapes[0][0] * shapes[0][1]
    assert all(b * c == rows for b, c in shapes) and rows * T * len(shapes) == n_tok
    tiles_in = S5_WIDTH // LANES
    width = T * S5_GROUP
    u_view = proj[:, COL_U:COL_U + S5_WIDTH].reshape(n_tok // T, T * S5_WIDTH)

    def in_map(s):
        return lambda r, o: (r, s * tiles_in + o)

    op_spec = pl.BlockSpec((GROUPS_PER_TILE, width, width), lambda r, o: (o, 0, 0))
    return pl.pallas_call(
        functools.partial(_s5_core_kernel, shapes=tuple(shapes)),
        grid=(len(shapes), S5_GROUPS // GROUPS_PER_TILE),
        in_specs=[pl.BlockSpec((rows, LANES), in_map(s)) for s in range(T)]
        + [op_spec, op_spec, op_spec, pl.BlockSpec((1, 2, GROUPS_PER_TILE, LANES), lambda r, o: (o, 0, 0, 0)),
           pl.BlockSpec((1, 1, LANES), lambda r, o: (o, 0, 0))],
        out_specs=pl.BlockSpec((T, rows, LANES), lambda r, o: (0, r, o)),
        out_shape=jax.ShapeDtypeStruct((T, n_tok // T, S5_WIDTH), BF16),
        scratch_shapes=[pltpu.VMEM((rows, GROUPS_PER_TILE * width), BF16),
                        pltpu.VMEM((rows, GROUPS_PER_TILE * width), BF16),
                        ] + [pltpu.VMEM((GROUPS_PER_TILE * rows, LANES), F32)] * 4,
        compiler_params=_cparams(("parallel", "parallel")),
    )(*([u_view] * T), w_op, m_op, v_op,
      a_op.reshape(S5_GROUPS // GROUPS_PER_TILE, GROUPS_PER_TILE, 2, LANES).transpose(0, 2, 1, 3),
      d_skip.reshape(S5_WIDTH // LANES, 1, LANES))


def _s5_post_kernel(y_ref, wg_ref, bg_ref, z_ref):
    y = y_ref[0].astype(F32)
    act = 0.5 * y * (1.0 + jnp.tanh(math.sqrt(2.0 / math.pi) * (y + 0.044715 * (y * y * y))))
    gl = _dot(act.astype(BF16), wg_ref[...]) + bg_ref[...]
    z_ref[...] = (act * _sigmoid(gl)).astype(z_ref.dtype)


def _s5_post(y_planes, wg, bg, bm):
    planes, rows, w = y_planes.shape
    z = pl.pallas_call(
        _s5_post_kernel,
        grid=(rows // bm, planes),
        in_specs=[pl.BlockSpec((1, bm, w), lambda i, t: (t, i, 0)),
                  pl.BlockSpec((w, w), lambda i, t: (0, 0)),
                  pl.BlockSpec((1, w), lambda i, t: (0, 0))],
        out_specs=pl.BlockSpec((bm, w), lambda i, t: (i, t)),
        out_shape=jax.ShapeDtypeStruct((rows, planes * w), BF16),
        compiler_params=_cparams(("parallel", "parallel")),
    )(y_planes, wg, bg.reshape(1, w))
    return z.reshape(rows * planes, w)


def _rms_norm(x, g):
    return x * lax.rsqrt(jnp.mean(x * x, axis=-1, keepdims=True) + RMS_EPS) * g


def _rope_block(r, cos, sin):
    return r * cos + pltpu.roll(r, LANES // 2, 1) * sin


def _qproj_kernel(cq_ref, g_ref, w_ref, cos_ref, sin_ref, q_ref, *, heads, scale):
    xn = _rms_norm(cq_ref[...].astype(F32), g_ref[...]).astype(BF16)
    acc = _dot(xn, w_ref[...])
    cos, sin = cos_ref[...], sin_ref[...]
    for h in range(heads):
        lo = h * HEAD_PAD
        q_ref[:, lo:lo + LANES] = (acc[:, lo:lo + LANES] * scale).astype(q_ref.dtype)
        rope = _rope_block(acc[:, lo + LANES:lo + HEAD_PAD], cos, sin)
        q_ref[:, lo + LANES:lo + HEAD_PAD] = (rope * scale).astype(q_ref.dtype)


def _qproj(proj, g, w, cos, sin, bm, pos_block, heads_per_step=4):
    n = proj.shape[0]
    nq = w.shape[1]
    bn = heads_per_step * HEAD_PAD
    scale = (QK_NOPE + QK_ROPE) ** -0.5 * math.log2(math.e)
    return pl.pallas_call(
        functools.partial(_qproj_kernel, heads=heads_per_step, scale=scale),
        grid=(n // bm, nq // bn),
        in_specs=[pl.BlockSpec((bm, Q_LORA), lambda i, j: (i, COL_CQ // Q_LORA)),
                  pl.BlockSpec((1, Q_LORA), lambda i, j: (0, 0)),
                  pl.BlockSpec((Q_LORA, bn), lambda i, j: (0, j)),
                  pl.BlockSpec((bm, LANES), lambda i, j: (pos_block(i), 0)),
                  pl.BlockSpec((bm, LANES), lambda i, j: (pos_block(i), 0))],
        out_specs=pl.BlockSpec((bm, bn), lambda i, j: (i, j)),
        out_shape=jax.ShapeDtypeStruct((n, nq), BF16),
        compiler_params=_cparams(("parallel", "parallel")),
    )(proj, g.reshape(1, Q_LORA), w, cos, sin)


def _kvproj_kernel(ckv_ref, kr_ref, g_ref, wk_ref, wv_ref, cos_ref, sin_ref, k_ref, v_ref):
    xn = _rms_norm(ckv_ref[...].astype(F32), g_ref[...]).astype(BF16)
    kn = _dot(xn, wk_ref[...])
    vv = _dot(xn, wv_ref[...])
    kro = _rope_block(kr_ref[...].astype(F32), cos_ref[...], sin_ref[...]).astype(k_ref.dtype)
    ones = (lax.broadcasted_iota(jnp.int32, kro.shape, 1) == 0).astype(v_ref.dtype)
    for h in range(MLA_HEADS):
        lo = h * HEAD_PAD
        k_ref[:, lo:lo + LANES] = kn[:, h * QK_NOPE:(h + 1) * QK_NOPE].astype(k_ref.dtype)
        k_ref[:, lo + LANES:lo + HEAD_PAD] = kro
        v_ref[:, 2 * h * V_HEAD:(2 * h + 1) * V_HEAD] = vv[:, h * V_HEAD:(h + 1) * V_HEAD].astype(v_ref.dtype)
        v_ref[:, (2 * h + 1) * V_HEAD:(2 * h + 2) * V_HEAD] = ones


def _kvproj(proj, g, wk, wv, cos, sin, bm, pos_block):
    n = proj.shape[0]
    return pl.pallas_call(
        _kvproj_kernel,
        grid=(n // bm,),
        in_specs=[pl.BlockSpec((bm, KV_LORA), lambda i: (i, COL_CKV // KV_LORA)),
                  pl.BlockSpec((bm, LANES), lambda i: (i, COL_KR // LANES)),
                  pl.BlockSpec((1, KV_LORA), lambda i: (0, 0)),
                  pl.BlockSpec(wk.shape, lambda i: (0, 0)),
                  pl.BlockSpec(wv.shape, lambda i: (0, 0)),
                  pl.BlockSpec((bm, LANES), lambda i: (pos_block(i), 0)),
                  pl.BlockSpec((bm, LANES), lambda i: (pos_block(i), 0))],
        out_specs=[pl.BlockSpec((bm, MLA_HEADS * HEAD_PAD), lambda i: (i, 0)),
                   pl.BlockSpec((bm, MLA_HEADS * 2 * V_HEAD), lambda i: (i, 0))],
        out_shape=[jax.ShapeDtypeStruct((n, MLA_HEADS * HEAD_PAD), BF16),
                   jax.ShapeDtypeStruct((n, MLA_HEADS * 2 * V_HEAD), BF16)],
        compiler_params=_cparams(("parallel",)),
    )(proj, proj, g.reshape(1, KV_LORA), wk, wv, cos, sin)


def _attn_kernel(q_ref, k_ref, v_ref, o_ref, *, n_chunks, tk):
    q = q_ref[...]

    def scores(c):
        return lax.dot_general(q, k_ref[c * tk:(c + 1) * tk, :], (((1,), (1,)), ((), ())),
                               preferred_element_type=F32)

    m = jnp.full((q.shape[0], 1), -jnp.inf, F32)
    acc = jnp.zeros((q.shape[0], 2 * V_HEAD), F32)
    s = scores(0)
    for c in range(n_chunks):
        s_next = scores(c + 1) if c + 1 < n_chunks else None
        m_new = jnp.maximum(m, jnp.max(s, axis=-1, keepdims=True))
        p = jnp.exp2((s - m_new).astype(BF16))
        acc = jnp.exp2(m - m_new) * acc + _dot(p, v_ref[c * tk:(c + 1) * tk, :])
        m, s = m_new, s_next
    o_ref[...] = (acc[:, :V_HEAD] / acc[:, V_HEAD:V_HEAD + 1]).astype(o_ref.dtype)


def _attention(q, k, v, base, batch, length, tq):
    nq = length // tq
    row0 = base // tq
    seq0 = base // length
    tk = min(1024, length)
    return pl.pallas_call(
        functools.partial(_attn_kernel, n_chunks=length // tk, tk=tk),
        grid=(batch, MLA_HEADS, nq),
        in_specs=[pl.BlockSpec((tq, HEAD_PAD), lambda b, h, i: (row0 + b * nq + i, h)),
                  pl.BlockSpec((length, HEAD_PAD), lambda b, h, i: (seq0 + b, h)),
                  pl.BlockSpec((length, 2 * V_HEAD), lambda b, h, i: (seq0 + b, h))],
        out_specs=pl.BlockSpec((tq, V_HEAD), lambda b, h, i: (b * nq + i, h)),
        out_shape=jax.ShapeDtypeStruct((batch * length, MLA_HEADS * V_HEAD), BF16),
        compiler_params=_cparams(("parallel", "parallel", "arbitrary")),
    )(q, k, v)


def _merge_kernel(z_ref, o1_ref, o2_ref, gs_ref, gm_ref, wsp_ref, wmp_ref, out_ref, *, split):
    ys = _dot(z_ref[...], wsp_ref[...])
    o = jnp.where(pl.program_id(0) < split, o1_ref[...], o2_ref[...])
    ym = _dot(o, wmp_ref[...])
    out = _sigmoid(gs_ref[...].astype(F32)) * ys + _sigmoid(gm_ref[...].astype(F32)) * ym
    out_ref[...] = out.astype(out_ref.dtype)


def _merge(z, o1, o2, proj, wsp, wmp, bm):
    n = z.shape[0]
    d = D_MODEL
    split = o1.shape[0] // bm
    return pl.pallas_call(
        functools.partial(_merge_kernel, split=split),
        grid=(n // bm,),
        in_specs=[pl.BlockSpec((bm, S5_WIDTH), lambda i: (i, 0)),
                  pl.BlockSpec((bm, d), lambda i: (jnp.minimum(i, split - 1), 0)),
                  pl.BlockSpec((bm, d), lambda i: (jnp.maximum(i - split, 0), 0)),
                  pl.BlockSpec((bm, d), lambda i: (i, COL_GS // d)),
                  pl.BlockSpec((bm, d), lambda i: (i, COL_GM // d)),
                  pl.BlockSpec(wsp.shape, lambda i: (0, 0)),
                  pl.BlockSpec(wmp.shape, lambda i: (0, 0))],
        out_specs=pl.BlockSpec((bm, d), lambda i: (i, 0)),
        out_shape=jax.ShapeDtypeStruct((n, d), BF16),
        compiler_params=_cparams(("parallel",)),
    )(z, o1, o2, proj, proj, wsp, wmp)


def _out_kernel(mg_ref, x_ref, mod_ref, wo_ref, lng_ref, lnb_ref, wr_ref, xo_ref, h2_ref, aff_ref):
    m = mod_ref[0]
    out = _dot(mg_ref[...], wo_ref[...])
    xn = _layer_norm(DEEPNORM_ALPHA * x_ref[...] + m[2:3] * out, lng_ref[...], lnb_ref[...])
    xo_ref[...] = xn
    h2 = (xn * (1.0 + m[4:5]) + m[3:4]).astype(BF16)
    h2_ref[...] = h2
    logits = _dot(h2, wr_ref[...])
    lane = lax.broadcasted_iota(jnp.int32, logits.shape, 1)
    logits = jnp.where(lane < N_EXPERTS, logits, -jnp.inf)
    e = jnp.exp(logits - jnp.max(logits, axis=-1, keepdims=True))
    aff_ref[...] = e / jnp.sum(e, axis=-1, keepdims=True)


def _out_proj(merged, x, mod, wo, lng, lnb, wr, bm, seg):
    n, d = x.shape
    return pl.pallas_call(
        _out_kernel,
        grid=(n // bm,),
        in_specs=[pl.BlockSpec((bm, d), lambda i: (i, 0)),
                  pl.BlockSpec((bm, d), lambda i: (i, 0)),
                  pl.BlockSpec((1, 6, d), lambda i: ((i * bm) // seg, 0, 0)),
                  pl.BlockSpec((d, d), lambda i: (0, 0)),
                  pl.BlockSpec((1, d), lambda i: (0, 0)),
                  pl.BlockSpec((1, d), lambda i: (0, 0)),
                  pl.BlockSpec((d, LANES), lambda i: (0, 0))],
        out_specs=[pl.BlockSpec((bm, d), lambda i: (i, 0)),
                   pl.BlockSpec((bm, d), lambda i: (i, 0)),
                   pl.BlockSpec((bm, LANES), lambda i: (i, 0))],
        out_shape=[jax.ShapeDtypeStruct((n, d), F32),
                   jax.ShapeDtypeStruct((n, d), BF16),
                   jax.ShapeDtypeStruct((n, LANES), F32)],
        compiler_params=_cparams(("parallel",)),
    )(merged, x, mod, wo, lng.reshape(1, d), lnb.reshape(1, d), wr)


def _select_kernel(aff_ref, gate_ref, cnt_ref, off_ref, *, n, cap, bm):
    aff = aff_ref[...]
    bits = pltpu.bitcast(aff, jnp.int32)
    idx = lax.broadcasted_iota(jnp.int32, aff.shape, 0)
    valid = lax.broadcasted_iota(jnp.int32, (1, LANES), 1) < N_EXPERTS

    def count(mask):
        return jnp.sum(mask.astype(jnp.int32), axis=0, keepdims=True)

    def thr_step(k, thr):
        cand = thr | jnp.left_shift(jnp.int32(1), 30 - k)
        return jnp.where(count(bits >= cand) >= cap, cand, thr)

    thr = lax.fori_loop(0, 31, thr_step, jnp.zeros((1, LANES), jnp.int32))
    above = bits > thr
    tied = bits == thr
    need = cap - count(above)
    idx_bits = max(1, (n - 1).bit_length())

    def idx_step(k, bound):
        cand = bound | jnp.left_shift(jnp.int32(1), idx_bits - 1 - k)
        return jnp.where(count(tied & (idx < cand)) < need, cand, bound)

    bound = lax.fori_loop(0, idx_bits, idx_step, jnp.zeros((1, LANES), jnp.int32))
    sel = (above | (tied & (idx <= bound))) & valid
    gate_ref[...] = jnp.where(sel, aff, -1.0)
    nt = n // bm
    cnt = jnp.sum(sel.astype(F32).reshape(nt, bm, LANES), axis=1)
    r = lax.broadcasted_iota(jnp.int32, (nt, nt), 0)
    c = lax.broadcasted_iota(jnp.int32, (nt, nt), 1)
    off = _dot((c < r).astype(BF16), cnt.astype(BF16))
    cnt_ref[0] = cnt.astype(jnp.int32)
    off_ref[0] = off.astype(jnp.int32)


def _select(aff, n_sets, n, cap, bm):
    nt = n // bm
    return pl.pallas_call(
        functools.partial(_select_kernel, n=n, cap=cap, bm=bm),
        grid=(n_sets,),
        in_specs=[pl.BlockSpec((n, LANES), lambda g: (g, 0))],
        out_specs=[pl.BlockSpec((n, LANES), lambda g: (g, 0)),
                   pl.BlockSpec((1, nt, LANES), lambda g: (g, 0, 0)),
                   pl.BlockSpec((1, nt, LANES), lambda g: (g, 0, 0))],
        out_shape=[jax.ShapeDtypeStruct((n_sets * n, LANES), F32),
                   jax.ShapeDtypeStruct((n_sets, nt, LANES), jnp.int32),
                   jax.ShapeDtypeStruct((n_sets, nt, LANES), jnp.int32)],
        compiler_params=_cparams(("parallel",)),
    )(aff)


def _replication_matrices():
    block = jnp.arange(N_EXPERTS * MOE_SLAB) // MOE_SLAB
    r1 = (jnp.arange(LANES)[:, None] == block[None, :]).astype(BF16)
    return r1, jnp.concatenate([SLOT_ALIGN * r1, r1], axis=0)


def _slot_onehot(gate, pos, offs, n_pass, shifts, r1_ref, r2_ref, weighted):
    lane = lax.broadcasted_iota(jnp.int32, (1, LANES), 1)
    col = lax.broadcasted_iota(jnp.int32, (1, N_EXPERTS * MOE_SLAB), 1)
    rem = jnp.zeros((1, LANES), F32)
    for e in range(N_EXPERTS):
        rem = jnp.where(lane == e, (offs[e] - _align_down(offs[e])).astype(F32), rem)
    rel = pos + rem
    hi = jnp.floor(rel * (1.0 / SLOT_ALIGN))
    digits = jnp.concatenate([hi, rel - SLOT_ALIGN * hi], axis=1).astype(BF16)
    rel_rep = _dot(digits, r2_ref[...])
    gate_rep = _dot(gate.astype(BF16), r1_ref[...])
    base = n_pass * MOE_SLAB
    base = float(base) if isinstance(base, int) else base.astype(F32)
    row = (col % MOE_SLAB).astype(F32)
    if shifts is None:
        target = row + base
    else:
        shift_rep = jnp.zeros(row.shape, F32)
        for e in range(N_EXPERTS):
            shift_rep = jnp.where(col // MOE_SLAB == e, shifts[e].astype(F32), shift_rep)
        target = jnp.where(row >= shift_rep, row - shift_rep + base, -1.0)
    hit = (gate_rep >= 0.0) & (rel_rep == target)
    return jnp.where(hit, gate_rep if weighted else 1.0, 0.0).astype(BF16)


def _slot_positions(gate):
    bm = gate.shape[0]
    r = lax.broadcasted_iota(jnp.int32, (bm, bm), 0)
    c = lax.broadcasted_iota(jnp.int32, (bm, bm), 1)
    return _dot((c < r).astype(BF16), (gate >= 0.0).astype(BF16))


def _align_down(v):
    return (v // SLOT_ALIGN) * SLOT_ALIGN


def _aligned(v):
    return v if isinstance(v, int) else pl.multiple_of(v, SLOT_ALIGN)


def _compact_kernel(off_ref, npass_ref, coff_ref, nsub_ref, h_ref, gate_ref, r1_ref, r2_ref, xe_hbm,
                    slab_ref, carry_ref, sem, pending_ref, *, nt, cap, n_steps):
    g = pl.program_id(0)
    i = pl.program_id(1)
    step = g * nt + i
    buf = step % 2
    base = step * N_EXPERTS
    offs = [off_ref[base + e] for e in range(N_EXPERTS)]
    npass = [npass_ref[base + e] for e in range(N_EXPERTS)]

    def window_copy(which, e, lo):
        return pltpu.make_async_copy(
            slab_ref.at[which, pl.ds(e * MOE_SLAB, MOE_SLAB)],
            xe_hbm.at[g, e, pl.ds(_aligned(lo), MOE_SLAB)], sem.at[which, e])

    def drain(which):
        for e in range(N_EXPERTS):
            window_copy(which, e, 0).wait()

    @pl.when(step == 0)
    def _():
        pending_ref[0] = 0

    @pl.when(i == 0)
    def _():
        carry_ref[...] = jnp.zeros_like(carry_ref)
        slab_ref[buf] = jnp.zeros(slab_ref.shape[1:], slab_ref.dtype)
        for e in range(N_EXPERTS):
            window_copy(buf, e, cap).start()
        drain(buf)

    gate = gate_ref[...]
    pos = _slot_positions(gate)

    def build(s):
        lo = [_align_down(offs[e]) + s * MOE_SLAB for e in range(N_EXPERTS)]
        sel_t = _slot_onehot(gate, pos, offs, s, None, r1_ref, r2_ref, weighted=False)
        slab = lax.dot_general(sel_t, h_ref[...], (((0,), (0,)), ((), ())), preferred_element_type=F32)
        slab_ref[buf] = slab.astype(slab_ref.dtype)
        return lo

    def keep_tail(s):
        for e in range(N_EXPERTS):
            @pl.when(s == npass[e] - 1)
            def _(e=e):
                src = pl.multiple_of(e * MOE_SLAB + coff_ref[base + e], SLOT_ALIGN)
                carry_ref[pl.ds(e * SLOT_ALIGN, SLOT_ALIGN), :] = slab_ref[buf, pl.ds(src, SLOT_ALIGN), :]

    lo0 = build(0)
    for e in range(N_EXPERTS):
        rows = pl.ds(e * MOE_SLAB, SLOT_ALIGN)
        merged = slab_ref[buf, rows, :].astype(F32) + carry_ref[pl.ds(e * SLOT_ALIGN, SLOT_ALIGN), :].astype(F32)
        slab_ref[buf, rows, :] = merged.astype(slab_ref.dtype)
    keep_tail(0)

    @pl.when(pending_ref[0] == 1)
    def _():
        drain(1 - buf)

    for e in range(N_EXPERTS):
        window_copy(buf, e, lo0[e]).start()
    pending_ref[0] = 1

    n_pass = nsub_ref[step]

    @pl.when((n_pass > 1) | (step == n_steps - 1))
    def _():
        drain(buf)
        pending_ref[0] = 0

    def extra_pass(s, carry):
        lo = build(s)
        keep_tail(s)
        for e in range(N_EXPERTS):
            @pl.when(s < npass[e])
            def _(e=e):
                window_copy(buf, e, lo[e]).start()
        for e in range(N_EXPERTS):
            @pl.when(s < npass[e])
            def _(e=e):
                window_copy(buf, e, lo[e]).wait()
        return carry

    lax.fori_loop(1, n_pass, extra_pass, 0)


def _compact(off, npass, coff, nsub, h2, gate, reps, n_sets, n, cap, bm):
    nt = n // bm
    d = h2.shape[1]
    grid_spec = pltpu.PrefetchScalarGridSpec(
        num_scalar_prefetch=4,
        grid=(n_sets, nt),
        in_specs=[pl.BlockSpec((bm, d), lambda g, i, *_: (g * nt + i, 0)),
                  pl.BlockSpec((bm, LANES), lambda g, i, *_: (g * nt + i, 0)),
                  pl.BlockSpec(reps[0].shape, lambda g, i, *_: (0, 0)),
                  pl.BlockSpec(reps[1].shape, lambda g, i, *_: (0, 0))],
        out_specs=pl.BlockSpec(memory_space=pl.ANY),
        scratch_shapes=[pltpu.VMEM((2, N_EXPERTS * MOE_SLAB, d), BF16),
                        pltpu.VMEM((N_EXPERTS * SLOT_ALIGN, d), BF16),
                        pltpu.SemaphoreType.DMA((2, N_EXPERTS)),
                        pltpu.SMEM((1,), jnp.int32)],
    )
    return pl.pallas_call(
        functools.partial(_compact_kernel, nt=nt, cap=cap, n_steps=n_sets * nt),
        grid_spec=grid_spec,
        out_shape=jax.ShapeDtypeStruct((n_sets, N_EXPERTS, cap + MOE_SLAB, d), BF16),
        compiler_params=_cparams(("arbitrary", "arbitrary")),
    )(off, npass, coff, nsub, h2, gate, *reps)


def _ffn_kernel(xe_ref, w1_ref, w3_ref, w2_ref, ye_ref):
    x = xe_ref[0, 0]
    a = _dot(x, w1_ref[0])
    b = _dot(x, w3_ref[0])
    hid = (a * _sigmoid(a) * b).astype(BF16)
    ye_ref[0, 0] = _dot(hid, w2_ref[0]).astype(ye_ref.dtype)


def _ffn(xe, w1, w3, w2, n_sets, cap, tm):
    d = xe.shape[-1]
    f = w1.shape[-1]
    return pl.pallas_call(
        _ffn_kernel,
        grid=(N_EXPERTS, n_sets, cap // tm),
        in_specs=[pl.BlockSpec((1, 1, tm, d), lambda e, g, t: (g, e, t, 0)),
                  pl.BlockSpec((1, d, f), lambda e, g, t: (e, 0, 0)),
                  pl.BlockSpec((1, d, f), lambda e, g, t: (e, 0, 0)),
                  pl.BlockSpec((1, f, d), lambda e, g, t: (e, 0, 0))],
        out_specs=pl.BlockSpec((1, 1, tm, d), lambda e, g, t: (g, e, t, 0)),
        out_shape=jax.ShapeDtypeStruct((n_sets, N_EXPERTS, cap, d), BF16),
        compiler_params=_cparams(("parallel", "parallel", "parallel")),
    )(xe, w1, w3, w2)


def _combine_kernel(off_ref, nsub_ref, gate_ref, x_ref, mod_ref, lng_ref, lnb_ref, r1_ref, r2_ref, ye_hbm,
                    xo_ref, slab_ref, sem, *, nt, cap, n_steps):
    g = pl.program_id(0)
    i = pl.program_id(1)
    step = g * nt + i
    buf = step % 2
    gate = gate_ref[...]
    pos = _slot_positions(gate)

    def windows(at_step, s):
        offs = [off_ref[at_step * N_EXPERTS + e] for e in range(N_EXPERTS)]
        lo = [_align_down(offs[e]) + s * MOE_SLAB for e in range(N_EXPERTS)]
        start = [jnp.minimum(lo[e], cap - MOE_SLAB) for e in range(N_EXPERTS)]
        return offs, lo, start

    def fetch(which, at_step, s):
        _, _, start = windows(at_step, s)
        set_idx = at_step // nt
        return [pltpu.make_async_copy(
            ye_hbm.at[set_idx, e, pl.ds(_aligned(start[e]), MOE_SLAB)],
            slab_ref.at[which, pl.ds(e * MOE_SLAB, MOE_SLAB)], sem.at[which, e]) for e in range(N_EXPERTS)]

    def spread_dot(which, s):
        offs, lo, start = windows(step, s)
        shifts = [lo[e] - start[e] for e in range(N_EXPERTS)]
        spread = _slot_onehot(gate, pos, offs, s, shifts, r1_ref, r2_ref, weighted=True)
        return _dot(spread, slab_ref[which])

    @pl.when(step == 0)
    def _():
        for cp in fetch(buf, step, 0):
            cp.start()

    @pl.when(step + 1 < n_steps)
    def _():
        for cp in fetch(1 - buf, step + 1, 0):
            cp.start()

    for cp in fetch(buf, step, 0):
        cp.wait()
    moe = spread_dot(buf, 0)

    def extra_pass(s, acc):
        copies = fetch(buf, step, s)
        for cp in copies:
            cp.start()
        for cp in copies:
            cp.wait()
        return acc + spread_dot(buf, s)

    moe = lax.fori_loop(1, nsub_ref[step], extra_pass, moe)
    m = mod_ref[0]
    xo_ref[...] = _layer_norm(DEEPNORM_ALPHA * x_ref[...] + m[5:6] * moe, lng_ref[...], lnb_ref[...])


def _combine(off, nsub, gate, x, mod, lng, lnb, reps, ye, n_sets, n, cap, bm, seg):
    nt = n // bm
    d = x.shape[1]
    grid_spec = pltpu.PrefetchScalarGridSpec(
        num_scalar_prefetch=2,
        grid=(n_sets, nt),
        in_specs=[pl.BlockSpec((bm, LANES), lambda g, i, *_: (g * nt + i, 0)),
                  pl.BlockSpec((bm, d), lambda g, i, *_: (g * nt + i, 0)),
                  pl.BlockSpec((1, 6, d), lambda g, i, *_: (((g * nt + i) * bm) // seg, 0, 0)),
                  pl.BlockSpec((1, d), lambda g, i, *_: (0, 0)),
                  pl.BlockSpec((1, d), lambda g, i, *_: (0, 0)),
                  pl.BlockSpec(reps[0].shape, lambda g, i, *_: (0, 0)),
                  pl.BlockSpec(reps[1].shape, lambda g, i, *_: (0, 0)),
                  pl.BlockSpec(memory_space=pl.ANY)],
        out_specs=pl.BlockSpec((bm, d), lambda g, i, *_: (g * nt + i, 0)),
        scratch_shapes=[pltpu.VMEM((2, N_EXPERTS * MOE_SLAB, d), BF16),
                        pltpu.SemaphoreType.DMA((2, N_EXPERTS))],
    )
    return pl.pallas_call(
        functools.partial(_combine_kernel, nt=nt, cap=cap, n_steps=n_sets * nt),
        grid_spec=grid_spec,
        out_shape=jax.ShapeDtypeStruct(x.shape, F32),
        compiler_params=_cparams(("arbitrary", "arbitrary")),
    )(off, nsub, gate, x, mod, lng.reshape(1, d), lnb.reshape(1, d), *reps, ye)


def _rope_lanes(w):
    z = jnp.zeros(w.shape[:-1] + (ROPE_HALF,), w.dtype)
    return jnp.concatenate([w[..., :ROPE_HALF], z, w[..., ROPE_HALF:], z], axis=-1)


def _proj_weight(w_in):
    c0 = S5_WIDTH + Q_LORA + KV_LORA
    kr = w_in[:, c0:c0 + QK_ROPE]
    gates = w_in[:, c0 + QK_ROPE:]
    return jnp.concatenate([w_in[:, :c0], gates, _rope_lanes(kr)], axis=1).astype(BF16)


def _q_weight(w_uq):
    w = w_uq.reshape(Q_LORA, MLA_HEADS, QK_NOPE + QK_ROPE)
    w = jnp.concatenate([w[..., :QK_NOPE], _rope_lanes(w[..., QK_NOPE:])], axis=-1)
    return w.reshape(Q_LORA, MLA_HEADS * HEAD_PAD).astype(BF16)


def _kv_weights(w_ukv):
    w = w_ukv.reshape(KV_LORA, MLA_HEADS, QK_NOPE + V_HEAD)
    wk = w[..., :QK_NOPE].reshape(KV_LORA, MLA_HEADS * QK_NOPE)
    wv = w[..., QK_NOPE:].reshape(KV_LORA, MLA_HEADS * V_HEAD)
    return wk.astype(BF16), wv.astype(BF16)


def _rope_tables(length):
    pos = jnp.arange(length, dtype=F32)
    inv = 1.0 / (ROPE_BASE ** (jnp.arange(0, QK_ROPE, 2, dtype=F32) / QK_ROPE))
    ang = pos[:, None] * inv[None, :]
    cos, sin = jnp.cos(ang), jnp.sin(ang)
    z = jnp.zeros_like(cos)
    return (jnp.concatenate([cos, z, cos, z], axis=1), jnp.concatenate([-sin, z, sin, z], axis=1))


def _trunk(x_groups, c_groups, p):
    (b1, l1, d), (b2, l2, _) = x_groups[0].shape, x_groups[1].shape
    n1, n2 = b1 * l1, b2 * l2
    n = n1 + n2
    seg = l1
    assert l2 % l1 == 0 and n1 == n2 and n1 % l2 == 0
    cap = EC_CAPACITY * n1 // N_EXPERTS
    bm = min(512, seg)
    bm_in = min(1024, seg)
    bm_moe = min(256, seg)
    tq = min(1024, l1)
    tm = min(256, cap)
    assert cap >= MOE_SLAB and cap % tm == 0

    x = jnp.concatenate([x_groups[0].reshape(n1, d), x_groups[1].reshape(n2, d)], axis=0)
    x = _ln_in(x, p['ln_in_g'], p['ln_in_b'], bm)

    c = jnp.concatenate(list(c_groups), axis=0)
    c_rows = -(-c.shape[0] // SUBLANES) * SUBLANES
    mod_all = _ada(jnp.pad(c, ((0, c_rows - c.shape[0]), (0, 0))), p['w_ada'], p['b_ada'])
    seg_rows = jnp.array(list(range(b1)) + [b1 + b for b in range(b2) for _ in range(l2 // seg)])
    mod_all = mod_all.reshape(DEPTH, c_rows, 6, d)[:, seg_rows]

    cos, sin = _rope_tables(max(l1, l2))
    reps = _replication_matrices()

    def pos_block(bm_):
        def f(i):
            return jnp.where(i < n1 // bm_, i % (l1 // bm_), (i - n1 // bm_) % (l2 // bm_))
        return f

    for l in range(DEPTH):
        mod = mod_all[l]
        proj = _inproj(x, mod, _proj_weight(p['w_in'][l]), bm_in, PROJ_W // 7, seg)

        m_op, w_op, v_op, a_op = _s5_operators(p['s5_lam_re'][l], p['s5_lam_im'][l], p['s5_log_step'][l],
                                               p['s5_b_re'][l], p['s5_b_im'][l], p['s5_c_re'][l], p['s5_c_im'][l])
        y_s5 = _s5_core(proj, w_op, m_op, v_op, a_op, p['s5_d'][l],
                        ((b1, l1 // S5_CHUNK), (b2, l2 // S5_CHUNK)))
        z = _s5_post(y_s5, p['s5_w_glu'][l].astype(BF16), p['s5_b_glu'][l], min(bm, n // S5_CHUNK))

        q = _qproj(proj, p['mla_q_norm'][l], _q_weight(p['mla_w_uq'][l]), cos, sin, bm, pos_block(bm))
        wk, wv = _kv_weights(p['mla_w_ukv'][l])
        k, v = _kvproj(proj, p['mla_kv_norm'][l], wk, wv, cos, sin, bm, pos_block(bm))
        o1 = _attention(q, k, v, 0, b1, l1, tq)
        o2 = _attention(q, k, v, n1, b2, l2, tq)

        merged = _merge(z, o1, o2, proj, p['w_s5_proj'][l].astype(BF16), p['w_mla_proj'][l].astype(BF16), bm)
        wr = jnp.pad(p['w_router'][l], ((0, 0), (0, LANES - N_EXPERTS))).astype(BF16)
        x, h2, aff = _out_proj(merged, x, mod, p['w_o'][l].astype(BF16), p['ln1_g'][l], p['ln1_b'][l],
                               wr, bm, seg)

        gate, cnt, off = _select(aff, 2, n1, cap, bm_moe)
        cnt, off = cnt[..., :N_EXPERTS], off[..., :N_EXPERTS]
        filled = off % SLOT_ALIGN + cnt
        npass = filled // MOE_SLAB + 1
        coff = (filled // SLOT_ALIGN) * SLOT_ALIGN - (npass - 1) * MOE_SLAB
        nsub = jnp.max(npass, axis=-1).reshape(-1)
        off, npass, coff = off.reshape(-1), npass.reshape(-1), coff.reshape(-1)
        xe = _compact(off, npass, coff, nsub, h2, gate, reps, 2, n1, cap, bm_moe)
        ye = _ffn(xe, p['w_exp1'][l].astype(BF16), p['w_exp3'][l].astype(BF16),
                  p['w_exp2'][l].astype(BF16), 2, cap, tm)
        x = _combine(off, nsub, gate, x, mod, p['ln2_g'][l], p['ln2_b'][l], reps, ye, 2, n1, cap, bm_moe, seg)

    return x[:n1].reshape(b1, l1, d), x[n1:].reshape(b2, l2, d)


def kernel(x_prompt, x_sample, c_prompt, c_sample, ln_in_g, ln_in_b, w_ada, b_ada, w_in, s5_lam_re, s5_lam_im, s5_log_step, s5_b_re, s5_b_im, s5_c_re, s5_c_im, s5_d, s5_w_glu, s5_b_glu, w_s5_proj, mla_q_norm, mla_w_uq, mla_kv_norm, mla_w_ukv, w_mla_proj, w_o, ln1_g, ln1_b, w_router, w_exp1, w_exp3, w_exp2, ln2_g, ln2_b):
    params = dict(ln_in_g=ln_in_g, ln_in_b=ln_in_b, w_ada=w_ada, b_ada=b_ada, w_in=w_in,
                  s5_lam_re=s5_lam_re, s5_lam_im=s5_lam_im, s5_log_step=s5_log_step,
                  s5_b_re=s5_b_re, s5_b_im=s5_b_im, s5_c_re=s5_c_re, s5_c_im=s5_c_im, s5_d=s5_d,
                  s5_w_glu=s5_w_glu, s5_b_glu=s5_b_glu, w_s5_proj=w_s5_proj,
                  mla_q_norm=mla_q_norm, mla_w_uq=mla_w_uq, mla_kv_norm=mla_kv_norm,
                  mla_w_ukv=mla_w_ukv, w_mla_proj=w_mla_proj, w_o=w_o, ln1_g=ln1_g, ln1_b=ln1_b,
                  w_router=w_router, w_exp1=w_exp1, w_exp3=w_exp3, w_exp2=w_exp2,
                  ln2_g=ln2_g, ln2_b=ln2_b)
    return _trunk((x_prompt, x_sample), (c_prompt, c_sample), params)
```

```python
import functools
import math

import jax
import jax.numpy as jnp
from jax import lax
from jax.experimental import pallas as pl
from jax.experimental.pallas import tpu as pltpu

F32 = jnp.float32
BF16 = jnp.bfloat16

D_MODEL = 2048
DEPTH = 4
S5_WIDTH = D_MODEL // 2
S5_GROUP = 16
S5_GROUPS = S5_WIDTH // S5_GROUP
S5_STATE = 64
MLA_HEADS = 16
QK_NOPE = 128
QK_ROPE = 64
V_HEAD = 128
Q_LORA = 512
KV_LORA = 512
ROPE_BASE = 10000.0
N_EXPERTS = 16
EC_CAPACITY = 2
D_EXPERT = 1408
DEEPNORM_ALPHA = (2 * DEPTH) ** 0.25
LN_EPS = 1e-5
RMS_EPS = 1e-6

LANES = 128
SUBLANES = 8
MXU_DIM = 256
S5_CHUNK = MXU_DIM // S5_GROUP
HEAD_PAD = 2 * LANES
ROPE_HALF = QK_ROPE // 2
COL_U, COL_CQ, COL_CKV = 0, S5_WIDTH, S5_WIDTH + Q_LORA
COL_GS = S5_WIDTH + Q_LORA + KV_LORA
COL_GM = COL_GS + D_MODEL
COL_KR = COL_GM + D_MODEL
PROJ_W = COL_KR + LANES
MOE_SLAB = 64
SLOT_ALIGN = 16
VMEM_LIMIT = 56 * 1024 * 1024


def _cparams(sem, vmem=VMEM_LIMIT):
    return pltpu.CompilerParams(dimension_semantics=sem, vmem_limit_bytes=vmem)


def _dot(a, b):
    return jnp.dot(a, b, preferred_element_type=F32)


def _layer_norm(y, g, b):
    mu = jnp.mean(y, axis=-1, keepdims=True)
    yc = y - mu
    var = jnp.mean(yc * yc, axis=-1, keepdims=True)
    return yc * lax.rsqrt(var + LN_EPS) * g + b


def _sigmoid(x):
    return 1.0 / (1.0 + jnp.exp(-x))


def _ln_in_kernel(x_ref, g_ref, b_ref, o_ref):
    o_ref[...] = _layer_norm(x_ref[...], g_ref[...], b_ref[...])


def _ln_in(x, g, b, bm):
    n, d = x.shape
    return pl.pallas_call(
        _ln_in_kernel,
        grid=(n // bm,),
        in_specs=[pl.BlockSpec((bm, d), lambda i: (i, 0)),
                  pl.BlockSpec((1, d), lambda i: (0, 0)),
                  pl.BlockSpec((1, d), lambda i: (0, 0))],
        out_specs=pl.BlockSpec((bm, d), lambda i: (i, 0)),
        out_shape=jax.ShapeDtypeStruct((n, d), F32),
        compiler_params=_cparams(("parallel",)),
    )(x, g.reshape(1, d), b.reshape(1, d))


def _ada_kernel(c_ref, w_ref, b_ref, o_ref):
    c = c_ref[...]
    cond = (c * _sigmoid(c)).astype(BF16)
    o_ref[0] = _dot(cond, w_ref[0].astype(BF16)) + b_ref[0]


def _ada(c_pad, w_ada, b_ada, bn=1024):
    depth, d, n6 = w_ada.shape
    rows = c_pad.shape[0]
    return pl.pallas_call(
        _ada_kernel,
        grid=(depth, n6 // bn),
        in_specs=[pl.BlockSpec((rows, d), lambda l, j: (0, 0)),
                  pl.BlockSpec((1, d, bn), lambda l, j: (l, 0, j)),
                  pl.BlockSpec((1, 1, bn), lambda l, j: (l, 0, j))],
        out_specs=pl.BlockSpec((1, rows, bn), lambda l, j: (l, 0, j)),
        out_shape=jax.ShapeDtypeStruct((depth, rows, n6), F32),
        compiler_params=_cparams(("parallel", "parallel")),
    )(c_pad, w_ada, b_ada.reshape(depth, 1, n6))


def _inproj_kernel(x_ref, mod_ref, w_ref, o_ref, h_ref):
    @pl.when(pl.program_id(1) == 0)
    def _():
        m = mod_ref[0]
        h_ref[...] = (x_ref[...] * (1.0 + m[1:2]) + m[0:1]).astype(BF16)

    o_ref[...] = _dot(h_ref[...], w_ref[...]).astype(o_ref.dtype)


def _inproj(x, mod, w, bm, bn, seg):
    n, d = x.shape
    nw = w.shape[1]
    return pl.pallas_call(
        _inproj_kernel,
        grid=(n // bm, nw // bn),
        in_specs=[pl.BlockSpec((bm, d), lambda i, j: (i, 0)),
                  pl.BlockSpec((1, 6, d), lambda i, j: ((i * bm) // seg, 0, 0)),
                  pl.BlockSpec((d, bn), lambda i, j: (0, j))],
        out_specs=pl.BlockSpec((bm, bn), lambda i, j: (i, j)),
        out_shape=jax.ShapeDtypeStruct((n, nw), BF16),
        scratch_shapes=[pltpu.VMEM((bm, d), BF16)],
        compiler_params=_cparams(("parallel", "arbitrary")),
    )(x, mod, w)


def _s5_operators(lam_re, lam_im, log_step, b_re, b_im, c_re, c_im):
    T = S5_CHUNK
    hi = lax.Precision.HIGHEST
    m_tot, w_cols, v_rows, a_re, a_im = 0.0, [], [], [], []
    for direction in range(2):
        lr = lam_re[direction].astype(F32)
        li = lam_im[direction].astype(F32)
        step = jnp.exp(log_step[direction].astype(F32))[:, None]
        mag = jnp.exp(lr * step)
        ar = mag * jnp.cos(li * step)
        ai = mag * jnp.sin(li * step)
        den = lr * lr + li * li
        fr = ((ar - 1.0) * lr + ai * li) / den
        fi = (ai * lr - (ar - 1.0) * li) / den
        br = b_re[direction].astype(F32)
        bi = b_im[direction].astype(F32)
        bbr = fr[..., None] * br - fi[..., None] * bi
        bbi = fr[..., None] * bi + fi[..., None] * br
        cr = c_re[direction].astype(F32)
        ci = c_im[direction].astype(F32)
        pr, pi = [jnp.ones_like(ar)], [jnp.zeros_like(ar)]
        for _ in range(T):
            pr.append(pr[-1] * ar - pi[-1] * ai)
            pi.append(pr[-2] * ai + pi[-1] * ar)
        pr = jnp.stack(pr)
        pi = jnp.stack(pi)
        abr = pr[:T, :, :, None] * bbr[None] - pi[:T, :, :, None] * bbi[None]
        abi = pr[:T, :, :, None] * bbi[None] + pi[:T, :, :, None] * bbr[None]
        kern = (jnp.einsum('ghp,tgpk->tghk', cr, abr, precision=hi)
                - jnp.einsum('ghp,tgpk->tghk', ci, abi, precision=hi))
        s_idx = jnp.arange(T)[:, None]
        t_idx = jnp.arange(T)[None, :]
        tau = (t_idx - s_idx) if direction == 0 else (s_idx - t_idx)
        ktab = kern[jnp.clip(tau, 0, T - 1)]
        ktab = jnp.where((tau >= 0)[:, :, None, None, None], ktab, 0.0)
        m_tot = m_tot + ktab.transpose(2, 0, 4, 1, 3).reshape(S5_GROUPS, T * S5_GROUP, T * S5_GROUP)
        order = jnp.arange(T - 1, -1, -1) if direction == 0 else jnp.arange(T)
        w_cols.append((abr[order].transpose(1, 0, 3, 2).reshape(S5_GROUPS, T * S5_GROUP, S5_STATE),
                       abi[order].transpose(1, 0, 3, 2).reshape(S5_GROUPS, T * S5_GROUP, S5_STATE)))
        k_idx = (jnp.arange(T) + 1) if direction == 0 else (T - jnp.arange(T))
        qr, qi = pr[k_idx], pi[k_idx]
        v_re = cr[None] * qr[:, :, None, :] - ci[None] * qi[:, :, None, :]
        v_im = -(cr[None] * qi[:, :, None, :] + ci[None] * qr[:, :, None, :])
        v_rows.append((v_re.transpose(1, 3, 0, 2).reshape(S5_GROUPS, S5_STATE, T * S5_GROUP),
                       v_im.transpose(1, 3, 0, 2).reshape(S5_GROUPS, S5_STATE, T * S5_GROUP)))
        a_re.append(pr[T])
        a_im.append(pi[T])
    w_op = jnp.concatenate([w_cols[0][0], w_cols[1][0], w_cols[0][1], w_cols[1][1]], axis=2)
    v_op = jnp.concatenate([v_rows[0][0], v_rows[1][0], v_rows[0][1], v_rows[1][1]], axis=1)
    a_op = jnp.stack([jnp.concatenate(a_re, axis=1), jnp.concatenate(a_im, axis=1)], axis=1)
    return m_tot.astype(BF16), w_op.astype(BF16), v_op.astype(BF16), a_op


GROUPS_PER_TILE = LANES // S5_GROUP
ROWS_PER_PACKED_VREG = 2 * SUBLANES


def _swap_atoms(tiles, n_out, n_in):
    atom = lax.broadcasted_iota(jnp.int32, tiles[0].shape, 1) // S5_GROUP
    out = []
    for o in range(n_out):
        acc = None
        for i in range(n_in):
            shift = ((i - o) % GROUPS_PER_TILE) * S5_GROUP
            piece = tiles[i] if shift == 0 else pltpu.roll(tiles[i], shift, 1)
            acc = piece if acc is None else jnp.where(atom == i, piece, acc)
        out.append(acc)
    return out


def _s5_core_kernel(*refs, shapes):
    T = S5_CHUNK
    u_refs, (w_ref, m_ref, v_ref, a_ref, d_ref) = refs[:T], refs[T:T + 5]
    y_ref, x_ref, yo_ref, s0_ref, s1_ref, h0_ref, h1_ref = refs[T + 5:]
    rows = x_ref.shape[0]
    half = S5_STATE
    width = T * S5_GROUP
    halves = width // LANES

    def gather_rows(i, carry):
        r0 = pl.multiple_of(i * ROWS_PER_PACKED_VREG, ROWS_PER_PACKED_VREG)
        src = [pltpu.bitcast(u_refs[s][pl.ds(r0, ROWS_PER_PACKED_VREG), :], jnp.uint32) for s in range(T)]
        for k in range(halves):
            per_group = _swap_atoms(src[k * GROUPS_PER_TILE:(k + 1) * GROUPS_PER_TILE], GROUPS_PER_TILE, GROUPS_PER_TILE)
            for g in range(GROUPS_PER_TILE):
                lo = g * width + k * LANES
                x_ref[pl.ds(r0, ROWS_PER_PACKED_VREG), lo:lo + LANES] = pltpu.bitcast(per_group[g], BF16)
        return carry

    lax.fori_loop(0, rows // ROWS_PER_PACKED_VREG, gather_rows, 0)

    for g in range(GROUPS_PER_TILE):
        s = _dot(x_ref[:, g * width:(g + 1) * width], w_ref[g])
        s0_ref[pl.ds(g, rows, stride=GROUPS_PER_TILE), :] = s[:, :LANES]
        s1_ref[pl.ds(g, rows, stride=GROUPS_PER_TILE), :] = s[:, LANES:]

    is_fwd = lax.broadcasted_iota(jnp.int32, (SUBLANES, LANES), 1) < half
    a_re, a_im = a_ref[0, 0], a_ref[0, 1]

    def scan(batch, n_chunks):
        def body(i, carry):
            new = []
            for b in range(batch):
                rf = pl.multiple_of((b * n_chunks + i) * SUBLANES, SUBLANES)
                rb = pl.multiple_of((b * n_chunks + n_chunks - 1 - i) * SUBLANES, SUBLANES)
                h0, h1 = carry[b]
                h0_ref[pl.ds(rf, SUBLANES), 0:half] = h0[:, :half]
                h1_ref[pl.ds(rf, SUBLANES), 0:half] = h1[:, :half]
                h0_ref[pl.ds(rb, SUBLANES), half:LANES] = h0[:, half:]
                h1_ref[pl.ds(rb, SUBLANES), half:LANES] = h1[:, half:]
                s0 = jnp.where(is_fwd, s0_ref[pl.ds(rf, SUBLANES), :], s0_ref[pl.ds(rb, SUBLANES), :])
                s1 = jnp.where(is_fwd, s1_ref[pl.ds(rf, SUBLANES), :], s1_ref[pl.ds(rb, SUBLANES), :])
                new.append((a_re * h0 - a_im * h1 + s0, a_re * h1 + a_im * h0 + s1))
            return tuple(new)

        zero = jnp.zeros((SUBLANES, LANES), F32)
        lax.fori_loop(0, n_chunks, body, tuple((zero, zero) for _ in range(batch)))

    for idx, (batch, n_chunks) in enumerate(shapes):
        @pl.when(pl.program_id(0) == idx)
        def _(batch=batch, n_chunks=n_chunks):
            scan(batch, n_chunks)

    for g in range(GROUPS_PER_TILE):
        xg = x_ref[:, g * width:(g + 1) * width]
        hg = jnp.concatenate([h0_ref[pl.ds(g, rows, stride=GROUPS_PER_TILE), :],
                              h1_ref[pl.ds(g, rows, stride=GROUPS_PER_TILE), :]], axis=1).astype(BF16)
        y = _dot(xg, m_ref[g]) + _dot(hg, v_ref[g])
        yo_ref[:, g * width:(g + 1) * width] = y.astype(yo_ref.dtype)

    def scatter_rows(i, carry):
        r0 = pl.multiple_of(i * ROWS_PER_PACKED_VREG, ROWS_PER_PACKED_VREG)
        for k in range(halves):
            src = [pltpu.bitcast(yo_ref[pl.ds(r0, ROWS_PER_PACKED_VREG), g * width + k * LANES:g * width + (k + 1) * LANES],
                                 jnp.uint32) for g in range(GROUPS_PER_TILE)]
            per_time = _swap_atoms(src, GROUPS_PER_TILE, GROUPS_PER_TILE)
            for j in range(GROUPS_PER_TILE):
                t = k * GROUPS_PER_TILE + j
                u_t = u_refs[t][pl.ds(r0, ROWS_PER_PACKED_VREG), :].astype(F32)
                y_t = pltpu.bitcast(per_time[j], BF16).astype(F32) + d_ref[0] * u_t
                y_ref[t, pl.ds(r0, ROWS_PER_PACKED_VREG), :] = y_t.astype(y_ref.dtype)
        return carry

    lax.fori_loop(0, rows // ROWS_PER_PACKED_VREG, scatter_rows, 0)


def _s5_core(proj, w_op, m_op, v_op, a_op, d_skip, shapes):
    T = S5_CHUNK
    n_tok, pw = proj.shape
    rows = shapes[0][0] * shapes[0][1]
    assert all(b * c == rows for b, c in shapes) and rows * T * len(shapes) == n_tok
    tiles_in = S5_WIDTH // LANES
    width = T * S5_GROUP
    u_view = proj[:, COL_U:COL_U + S5_WIDTH].reshape(n_tok // T, T * S5_WIDTH)

    def in_map(s):
        return lambda r, o: (r, s * tiles_in + o)

    op_spec = pl.BlockSpec((GROUPS_PER_TILE, width, width), lambda r, o: (o, 0, 0))
    return pl.pallas_call(
        functools.partial(_s5_core_kernel, shapes=tuple(shapes)),
        grid=(len(shapes), S5_GROUPS // GROUPS_PER_TILE),
        in_specs=[pl.BlockSpec((rows, LANES), in_map(s)) for s in range(T)]
        + [op_spec, op_spec, op_spec, pl.BlockSpec((1, 2, GROUPS_PER_TILE, LANES), lambda r, o: (o, 0, 0, 0)),
           pl.BlockSpec((1, 1, LANES), lambda r, o: (o, 0, 0))],
        out_specs=pl.BlockSpec((T, rows, LANES), lambda r, o: (0, r, o)),
        out_shape=jax.ShapeDtypeStruct((T, n_tok // T, S5_WIDTH), BF16),
        scratch_shapes=[pltpu.VMEM((rows, GROUPS_PER_TILE * width), BF16),
                        pltpu.VMEM((rows, GROUPS_PER_TILE * width), BF16),
                        ] + [pltpu.VMEM((GROUPS_PER_TILE * rows, LANES), F32)] * 4,
        compiler_params=_cparams(("parallel", "parallel")),
    )(*([u_view] * T), w_op, m_op, v_op,
      a_op.reshape(S5_GROUPS // GROUPS_PER_TILE, GROUPS_PER_TILE, 2, LANES).transpose(0, 2, 1, 3),
      d_skip.reshape(S5_WIDTH // LANES, 1, LANES))


def _s5_post_kernel(y_ref, wg_ref, bg_ref, z_ref):
    y = y_ref[0].astype(F32)
    act = 0.5 * y * (1.0 + jnp.tanh(math.sqrt(2.0 / math.pi) * (y + 0.044715 * (y * y * y))))
    gl = _dot(act.astype(BF16), wg_ref[...]) + bg_ref[...]
    z_ref[...] = (act * _sigmoid(gl)).astype(z_ref.dtype)


def _s5_post(y_planes, wg, bg, bm):
    planes, rows, w = y_planes.shape
    z = pl.pallas_call(
        _s5_post_kernel,
        grid=(rows // bm, planes),
        in_specs=[pl.BlockSpec((1, bm, w), lambda i, t: (t, i, 0)),
                  pl.BlockSpec((w, w), lambda i, t: (0, 0)),
                  pl.BlockSpec((1, w), lambda i, t: (0, 0))],
        out_specs=pl.BlockSpec((bm, w), lambda i, t: (i, t)),
        out_shape=jax.ShapeDtypeStruct((rows, planes * w), BF16),
        compiler_params=_cparams(("parallel", "parallel")),
    )(y_planes, wg, bg.reshape(1, w))
    return z.reshape(rows * planes, w)


def _rms_norm(x, g):
    return x * lax.rsqrt(jnp.mean(x * x, axis=-1, keepdims=True) + RMS_EPS) * g


def _rope_block(r, cos, sin):
    return r * cos + pltpu.roll(r, LANES // 2, 1) * sin


def _qproj_kernel(cq_ref, g_ref, w_ref, cos_ref, sin_ref, q_ref, *, heads, scale):
    xn = _rms_norm(cq_ref[...].astype(F32), g_ref[...]).astype(BF16)
    acc = _dot(xn, w_ref[...])
    cos, sin = cos_ref[...], sin_ref[...]
    for h in range(heads):
        lo = h * HEAD_PAD
        q_ref[:, lo:lo + LANES] = (acc[:, lo:lo + LANES] * scale).astype(q_ref.dtype)
        rope = _rope_block(acc[:, lo + LANES:lo + HEAD_PAD], cos, sin)
        q_ref[:, lo + LANES:lo + HEAD_PAD] = (rope * scale).astype(q_ref.dtype)


def _qproj(proj, g, w, cos, sin, bm, pos_block, heads_per_step=4):
    n = proj.shape[0]
    nq = w.shape[1]
    bn = heads_per_step * HEAD_PAD
    scale = (QK_NOPE + QK_ROPE) ** -0.5 * math.log2(math.e)
    return pl.pallas_call(
        functools.partial(_qproj_kernel, heads=heads_per_step, scale=scale),
        grid=(n // bm, nq // bn),
        in_specs=[pl.BlockSpec((bm, Q_LORA), lambda i, j: (i, COL_CQ // Q_LORA)),
                  pl.BlockSpec((1, Q_LORA), lambda i, j: (0, 0)),
                  pl.BlockSpec((Q_LORA, bn), lambda i, j: (0, j)),
                  pl.BlockSpec((bm, LANES), lambda i, j: (pos_block(i), 0)),
                  pl.BlockSpec((bm, LANES), lambda i, j: (pos_block(i), 0))],
        out_specs=pl.BlockSpec((bm, bn), lambda i, j: (i, j)),
        out_shape=jax.ShapeDtypeStruct((n, nq), BF16),
        compiler_params=_cparams(("parallel", "parallel")),
    )(proj, g.reshape(1, Q_LORA), w, cos, sin)


def _kvproj_kernel(ckv_ref, kr_ref, g_ref, wk_ref, wv_ref, cos_ref, sin_ref, k_ref, v_ref):
    xn = _rms_norm(ckv_ref[...].astype(F32), g_ref[...]).astype(BF16)
    kn = _dot(xn, wk_ref[...])
    vv = _dot(xn, wv_ref[...])
    kro = _rope_block(kr_ref[...].astype(F32), cos_ref[...], sin_ref[...]).astype(k_ref.dtype)
    ones = (lax.broadcasted_iota(jnp.int32, kro.shape, 1) == 0).astype(v_ref.dtype)
    for h in range(MLA_HEADS):
        lo = h * HEAD_PAD
        k_ref[:, lo:lo + LANES] = kn[:, h * QK_NOPE:(h + 1) * QK_NOPE].astype(k_ref.dtype)
        k_ref[:, lo + LANES:lo + HEAD_PAD] = kro
        v_ref[:, 2 * h * V_HEAD:(2 * h + 1) * V_HEAD] = vv[:, h * V_HEAD:(h + 1) * V_HEAD].astype(v_ref.dtype)
        v_ref[:, (2 * h + 1) * V_HEAD:(2 * h + 2) * V_HEAD] = ones


def _kvproj(proj, g, wk, wv, cos, sin, bm, pos_block):
    n = proj.shape[0]
    return pl.pallas_call(
        _kvproj_kernel,
        grid=(n // bm,),
        in_specs=[pl.BlockSpec((bm, KV_LORA), lambda i: (i, COL_CKV // KV_LORA)),
                  pl.BlockSpec((bm, LANES), lambda i: (i, COL_KR // LANES)),
                  pl.BlockSpec((1, KV_LORA), lambda i: (0, 0)),
                  pl.BlockSpec(wk.shape, lambda i: (0, 0)),
                  pl.BlockSpec(wv.shape, lambda i: (0, 0)),
                  pl.BlockSpec((bm, LANES), lambda i: (pos_block(i), 0)),
                  pl.BlockSpec((bm, LANES), lambda i: (pos_block(i), 0))],
        out_specs=[pl.BlockSpec((bm, MLA_HEADS * HEAD_PAD), lambda i: (i, 0)),
                   pl.BlockSpec((bm, MLA_HEADS * 2 * V_HEAD), lambda i: (i, 0))],
        out_shape=[jax.ShapeDtypeStruct((n, MLA_HEADS * HEAD_PAD), BF16),
                   jax.ShapeDtypeStruct((n, MLA_HEADS * 2 * V_HEAD), BF16)],
        compiler_params=_cparams(("parallel",)),
    )(proj, proj, g.reshape(1, KV_LORA), wk, wv, cos, sin)


def _attn_kernel(q_ref, k_ref, v_ref, o_ref, *, n_chunks, tk):
    q = q_ref[...]

    def scores(c):
        return lax.dot_general(q, k_ref[c * tk:(c + 1) * tk, :], (((1,), (1,)), ((), ())),
                               preferred_element_type=F32)

    m = jnp.full((q.shape[0], 1), -jnp.inf, F32)
    acc = jnp.zeros((q.shape[0], 2 * V_HEAD), F32)
    s = scores(0)
    for c in range(n_chunks):
        s_next = scores(c + 1) if c + 1 < n_chunks else None
        m_new = jnp.maximum(m, jnp.max(s, axis=-1, keepdims=True))
        p = jnp.exp2((s - m_new).astype(BF16))
        acc = jnp.exp2(m - m_new) * acc + _dot(p, v_ref[c * tk:(c + 1) * tk, :])
        m, s = m_new, s_next
    o_ref[...] = (acc[:, :V_HEAD] / acc[:, V_HEAD:V_HEAD + 1]).astype(o_ref.dtype)


def _attention(q, k, v, base, batch, length, tq):
    nq = length // tq
    row0 = base // tq
    seq0 = base // length
    tk = min(1024, length)
    return pl.pallas_call(
        functools.partial(_attn_kernel, n_chunks=length // tk, tk=tk),
        grid=(batch, MLA_HEADS, nq),
        in_specs=[pl.BlockSpec((tq, HEAD_PAD), lambda b, h, i: (row0 + b * nq + i, h)),
                  pl.BlockSpec((length, HEAD_PAD), lambda b, h, i: (seq0 + b, h)),
                  pl.BlockSpec((length, 2 * V_HEAD), lambda b, h, i: (seq0 + b, h))],
        out_specs=pl.BlockSpec((tq, V_HEAD), lambda b, h, i: (b * nq + i, h)),
        out_shape=jax.ShapeDtypeStruct((batch * length, MLA_HEADS * V_HEAD), BF16),
        compiler_params=_cparams(("parallel", "parallel", "arbitrary")),
    )(q, k, v)


def _merge_kernel(z_ref, o1_ref, o2_ref, gs_ref, gm_ref, wsp_ref, wmp_ref, out_ref, *, split):
    ys = _dot(z_ref[...], wsp_ref[...])
    o = jnp.where(pl.program_id(0) < split, o1_ref[...], o2_ref[...])
    ym = _dot(o, wmp_ref[...])
    out = _sigmoid(gs_ref[...].astype(F32)) * ys + _sigmoid(gm_ref[...].astype(F32)) * ym
    out_ref[...] = out.astype(out_ref.dtype)


def _merge(z, o1, o2, proj, wsp, wmp, bm):
    n = z.shape[0]
    d = D_MODEL
    split = o1.shape[0] // bm
    return pl.pallas_call(
        functools.partial(_merge_kernel, split=split),
        grid=(n // bm,),
        in_specs=[pl.BlockSpec((bm, S5_WIDTH), lambda i: (i, 0)),
                  pl.BlockSpec((bm, d), lambda i: (jnp.minimum(i, split - 1), 0)),
                  pl.BlockSpec((bm, d), lambda i: (jnp.maximum(i - split, 0), 0)),
                  pl.BlockSpec((bm, d), lambda i: (i, COL_GS // d)),
                  pl.BlockSpec((bm, d), lambda i: (i, COL_GM // d)),
                  pl.BlockSpec(wsp.shape, lambda i: (0, 0)),
                  pl.BlockSpec(wmp.shape, lambda i: (0, 0))],
        out_specs=pl.BlockSpec((bm, d), lambda i: (i, 0)),
        out_shape=jax.ShapeDtypeStruct((n, d), BF16),
        compiler_params=_cparams(("parallel",)),
    )(z, o1, o2, proj, proj, wsp, wmp)


def _out_kernel(mg_ref, x_ref, mod_ref, wo_ref, lng_ref, lnb_ref, wr_ref, xo_ref, h2_ref, aff_ref):
    m = mod_ref[0]
    out = _dot(mg_ref[...], wo_ref[...])
    xn = _layer_norm(DEEPNORM_ALPHA * x_ref[...] + m[2:3] * out, lng_ref[...], lnb_ref[...])
    xo_ref[...] = xn
    h2 = (xn * (1.0 + m[4:5]) + m[3:4]).astype(BF16)
    h2_ref[...] = h2
    logits = _dot(h2, wr_ref[...])
    lane = lax.broadcasted_iota(jnp.int32, logits.shape, 1)
    logits = jnp.where(lane < N_EXPERTS, logits, -jnp.inf)
    e = jnp.exp(logits - jnp.max(logits, axis=-1, keepdims=True))
    aff_ref[...] = e / jnp.sum(e, axis=-1, keepdims=True)


def _out_proj(merged, x, mod, wo, lng, lnb, wr, bm, seg):
    n, d = x.shape
    return pl.pallas_call(
        _out_kernel,
        grid=(n // bm,),
        in_specs=[pl.BlockSpec((bm, d), lambda i: (i, 0)),
                  pl.BlockSpec((bm, d), lambda i: (i, 0)),
                  pl.BlockSpec((1, 6, d), lambda i: ((i * bm) // seg, 0, 0)),
                  pl.BlockSpec((d, d), lambda i: (0, 0)),
                  pl.BlockSpec((1, d), lambda i: (0, 0)),
                  pl.BlockSpec((1, d), lambda i: (0, 0)),
                  pl.BlockSpec((d, LANES), lambda i: (0, 0))],
        out_specs=[pl.BlockSpec((bm, d), lambda i: (i, 0)),
                   pl.BlockSpec((bm, d), lambda i: (i, 0)),
                   pl.BlockSpec((bm, LANES), lambda i: (i, 0))],
        out_shape=[jax.ShapeDtypeStruct((n, d), F32),
                   jax.ShapeDtypeStruct((n, d), BF16),
                   jax.ShapeDtypeStruct((n, LANES), F32)],
        compiler_params=_cparams(("parallel",)),
    )(merged, x, mod, wo, lng.reshape(1, d), lnb.reshape(1, d), wr)


def _select_kernel(aff_ref, gate_ref, cnt_ref, off_ref, *, n, cap, bm):
    aff = aff_ref[...]
    bits = pltpu.bitcast(aff, jnp.int32)
    idx = lax.broadcasted_iota(jnp.int32, aff.shape, 0)
    valid = lax.broadcasted_iota(jnp.int32, (1, LANES), 1) < N_EXPERTS

    def count(mask):
        return jnp.sum(mask.astype(jnp.int32), axis=0, keepdims=True)

    def thr_step(k, thr):
        cand = thr | jnp.left_shift(jnp.int32(1), 30 - k)
        return jnp.where(count(bits >= cand) >= cap, cand, thr)

    thr = lax.fori_loop(0, 31, thr_step, jnp.zeros((1, LANES), jnp.int32))
    above = bits > thr
    tied = bits == thr
    need = cap - count(above)
    idx_bits = max(1, (n - 1).bit_length())

    def idx_step(k, bound):
        cand = bound | jnp.left_shift(jnp.int32(1), idx_bits - 1 - k)
        return jnp.where(count(tied & (idx < cand)) < need, cand, bound)

    bound = lax.fori_loop(0, idx_bits, idx_step, jnp.zeros((1, LANES), jnp.int32))
    sel = (above | (tied & (idx <= bound))) & valid
    gate_ref[...] = jnp.where(sel, aff, -1.0)
    nt = n // bm
    cnt = jnp.sum(sel.astype(F32).reshape(nt, bm, LANES), axis=1)
    r = lax.broadcasted_iota(jnp.int32, (nt, nt), 0)
    c = lax.broadcasted_iota(jnp.int32, (nt, nt), 1)
    off = _dot((c < r).astype(BF16), cnt.astype(BF16))
    cnt_ref[0] = cnt.astype(jnp.int32)
    off_ref[0] = off.astype(jnp.int32)


def _select(aff, n_sets, n, cap, bm):
    nt = n // bm
    return pl.pallas_call(
        functools.partial(_select_kernel, n=n, cap=cap, bm=bm),
        grid=(n_sets,),
        in_specs=[pl.BlockSpec((n, LANES), lambda g: (g, 0))],
        out_specs=[pl.BlockSpec((n, LANES), lambda g: (g, 0)),
                   pl.BlockSpec((1, nt, LANES), lambda g: (g, 0, 0)),
                   pl.BlockSpec((1, nt, LANES), lambda g: (g, 0, 0))],
        out_shape=[jax.ShapeDtypeStruct((n_sets * n, LANES), F32),
                   jax.ShapeDtypeStruct((n_sets, nt, LANES), jnp.int32),
                   jax.ShapeDtypeStruct((n_sets, nt, LANES), jnp.int32)],
        compiler_params=_cparams(("parallel",)),
    )(aff)


def _replication_matrices():
    block = jnp.arange(N_EXPERTS * MOE_SLAB) // MOE_SLAB
    r1 = (jnp.arange(LANES)[:, None] == block[None, :]).astype(BF16)
    return r1, jnp.concatenate([SLOT_ALIGN * r1, r1], axis=0)


def _slot_onehot(gate, pos, offs, n_pass, shifts, r1_ref, r2_ref, weighted):
    lane = lax.broadcasted_iota(jnp.int32, (1, LANES), 1)
    col = lax.broadcasted_iota(jnp.int32, (1, N_EXPERTS * MOE_SLAB), 1)
    rem = jnp.zeros((1, LANES), F32)
    for e in range(N_EXPERTS):
        rem = jnp.where(lane == e, (offs[e] - _align_down(offs[e])).astype(F32), rem)
    rel = pos + rem
    hi = jnp.floor(rel * (1.0 / SLOT_ALIGN))
    digits = jnp.concatenate([hi, rel - SLOT_ALIGN * hi], axis=1).astype(BF16)
    rel_rep = _dot(digits, r2_ref[...])
    gate_rep = _dot(gate.astype(BF16), r1_ref[...])
    base = n_pass * MOE_SLAB
    base = float(base) if isinstance(base, int) else base.astype(F32)
    row = (col % MOE_SLAB).astype(F32)
    if shifts is None:
        target = row + base
    else:
        shift_rep = jnp.zeros(row.shape, F32)
        for e in range(N_EXPERTS):
            shift_rep = jnp.where(col // MOE_SLAB == e, shifts[e].astype(F32), shift_rep)
        target = jnp.where(row >= shift_rep, row - shift_rep + base, -1.0)
    hit = (gate_rep >= 0.0) & (rel_rep == target)
    return jnp.where(hit, gate_rep if weighted else 1.0, 0.0).astype(BF16)


def _slot_positions(gate):
    bm = gate.shape[0]
    r = lax.broadcasted_iota(jnp.int32, (bm, bm), 0)
    c = lax.broadcasted_iota(jnp.int32, (bm, bm), 1)
    return _dot((c < r).astype(BF16), (gate >= 0.0).astype(BF16))


def _align_down(v):
    return (v // SLOT_ALIGN) * SLOT_ALIGN


def _aligned(v):
    return v if isinstance(v, int) else pl.multiple_of(v, SLOT_ALIGN)


def _compact_kernel(off_ref, npass_ref, coff_ref, nsub_ref, h_ref, gate_ref, r1_ref, r2_ref, xe_hbm,
                    slab_ref, carry_ref, sem, pending_ref, *, nt, cap, n_steps):
    g = pl.program_id(0)
    i = pl.program_id(1)
    step = g * nt + i
    buf = step % 2
    base = step * N_EXPERTS
    offs = [off_ref[base + e] for e in range(N_EXPERTS)]
    npass = [npass_ref[base + e] for e in range(N_EXPERTS)]

    def window_copy(which, e, lo):
        return pltpu.make_async_copy(
            slab_ref.at[which, pl.ds(e * MOE_SLAB, MOE_SLAB)],
            xe_hbm.at[g, e, pl.ds(_aligned(lo), MOE_SLAB)], sem.at[which, e])

    def drain(which):
        for e in range(N_EXPERTS):
            window_copy(which, e, 0).wait()

    @pl.when(step == 0)
    def _():
        pending_ref[0] = 0

    @pl.when(i == 0)
    def _():
        carry_ref[...] = jnp.zeros_like(carry_ref)
        slab_ref[buf] = jnp.zeros(slab_ref.shape[1:], slab_ref.dtype)
        for e in range(N_EXPERTS):
            window_copy(buf, e, cap).start()
        drain(buf)

    gate = gate_ref[...]
    pos = _slot_positions(gate)

    def build(s):
        lo = [_align_down(offs[e]) + s * MOE_SLAB for e in range(N_EXPERTS)]
        sel_t = _slot_onehot(gate, pos, offs, s, None, r1_ref, r2_ref, weighted=False)
        slab = lax.dot_general(sel_t, h_ref[...], (((0,), (0,)), ((), ())), preferred_element_type=F32)
        slab_ref[buf] = slab.astype(slab_ref.dtype)
        return lo

    def keep_tail(s):
        for e in range(N_EXPERTS):
            @pl.when(s == npass[e] - 1)
            def _(e=e):
                src = pl.multiple_of(e * MOE_SLAB + coff_ref[base + e], SLOT_ALIGN)
                carry_ref[pl.ds(e * SLOT_ALIGN, SLOT_ALIGN), :] = slab_ref[buf, pl.ds(src, SLOT_ALIGN), :]

    lo0 = build(0)
    for e in range(N_EXPERTS):
        rows = pl.ds(e * MOE_SLAB, SLOT_ALIGN)
        merged = slab_ref[buf, rows, :].astype(F32) + carry_ref[pl.ds(e * SLOT_ALIGN, SLOT_ALIGN), :].astype(F32)
        slab_ref[buf, rows, :] = merged.astype(slab_ref.dtype)
    keep_tail(0)

    @pl.when(pending_ref[0] == 1)
    def _():
        drain(1 - buf)

    for e in range(N_EXPERTS):
        window_copy(buf, e, lo0[e]).start(priority=e % 2)
    pending_ref[0] = 1

    n_pass = nsub_ref[step]

    @pl.when((n_pass > 1) | (step == n_steps - 1))
    def _():
        drain(buf)
        pending_ref[0] = 0

    def extra_pass(s, carry):
        lo = build(s)
        keep_tail(s)
        for e in range(N_EXPERTS):
            @pl.when(s < npass[e])
            def _(e=e):
                window_copy(buf, e, lo[e]).start()
        for e in range(N_EXPERTS):
            @pl.when(s < npass[e])
            def _(e=e):
                window_copy(buf, e, lo[e]).wait()
        return carry

    lax.fori_loop(1, n_pass, extra_pass, 0)


def _compact(off, npass, coff, nsub, h2, gate, reps, n_sets, n, cap, bm):
    nt = n // bm
    d = h2.shape[1]
    grid_spec = pltpu.PrefetchScalarGridSpec(
        num_scalar_prefetch=4,
        grid=(n_sets, nt),
        in_specs=[pl.BlockSpec((bm, d), lambda g, i, *_: (g * nt + i, 0)),
                  pl.BlockSpec((bm, LANES), lambda g, i, *_: (g * nt + i, 0)),
                  pl.BlockSpec(reps[0].shape, lambda g, i, *_: (0, 0)),
                  pl.BlockSpec(reps[1].shape, lambda g, i, *_: (0, 0))],
        out_specs=pl.BlockSpec(memory_space=pl.ANY),
        scratch_shapes=[pltpu.VMEM((2, N_EXPERTS * MOE_SLAB, d), BF16),
                        pltpu.VMEM((N_EXPERTS * SLOT_ALIGN, d), BF16),
                        pltpu.SemaphoreType.DMA((2, N_EXPERTS)),
                        pltpu.SMEM((1,), jnp.int32)],
    )
    return pl.pallas_call(
        functools.partial(_compact_kernel, nt=nt, cap=cap, n_steps=n_sets * nt),
        grid_spec=grid_spec,
        out_shape=jax.ShapeDtypeStruct((n_sets, N_EXPERTS, cap + MOE_SLAB, d), BF16),
        compiler_params=_cparams(("arbitrary", "arbitrary")),
    )(off, npass, coff, nsub, h2, gate, *reps)


def _ffn_kernel(xe_ref, w1_ref, w3_ref, w2_ref, ye_ref):
    x = xe_ref[0, 0]
    a = _dot(x, w1_ref[0])
    b = _dot(x, w3_ref[0])
    hid = (a * _sigmoid(a) * b).astype(BF16)
    ye_ref[0, 0] = _dot(hid, w2_ref[0]).astype(ye_ref.dtype)


def _ffn(xe, w1, w3, w2, n_sets, cap, tm):
    d = xe.shape[-1]
    f = w1.shape[-1]
    return pl.pallas_call(
        _ffn_kernel,
        grid=(N_EXPERTS, n_sets, cap // tm),
        in_specs=[pl.BlockSpec((1, 1, tm, d), lambda e, g, t: (g, e, t, 0)),
                  pl.BlockSpec((1, d, f), lambda e, g, t: (e, 0, 0)),
                  pl.BlockSpec((1, d, f), lambda e, g, t: (e, 0, 0)),
                  pl.BlockSpec((1, f, d), lambda e, g, t: (e, 0, 0))],
        out_specs=pl.BlockSpec((1, 1, tm, d), lambda e, g, t: (g, e, t, 0)),
        out_shape=jax.ShapeDtypeStruct((n_sets, N_EXPERTS, cap, d), BF16),
        compiler_params=_cparams(("parallel", "parallel", "parallel")),
    )(xe, w1, w3, w2)


def _combine_kernel(off_ref, nsub_ref, gate_ref, x_ref, mod_ref, lng_ref, lnb_ref, r1_ref, r2_ref, ye_hbm,
                    xo_ref, slab_ref, sem, *, nt, cap, n_steps):
    g = pl.program_id(0)
    i = pl.program_id(1)
    step = g * nt + i
    buf = step % 2
    gate = gate_ref[...]
    pos = _slot_positions(gate)

    def windows(at_step, s):
        offs = [off_ref[at_step * N_EXPERTS + e] for e in range(N_EXPERTS)]
        lo = [_align_down(offs[e]) + s * MOE_SLAB for e in range(N_EXPERTS)]
        start = [jnp.minimum(lo[e], cap - MOE_SLAB) for e in range(N_EXPERTS)]
        return offs, lo, start

    def fetch(which, at_step, s):
        _, _, start = windows(at_step, s)
        set_idx = at_step // nt
        return [pltpu.make_async_copy(
            ye_hbm.at[set_idx, e, pl.ds(_aligned(start[e]), MOE_SLAB)],
            slab_ref.at[which, pl.ds(e * MOE_SLAB, MOE_SLAB)], sem.at[which, e]) for e in range(N_EXPERTS)]

    def spread_dot(which, s):
        offs, lo, start = windows(step, s)
        shifts = [lo[e] - start[e] for e in range(N_EXPERTS)]
        spread = _slot_onehot(gate, pos, offs, s, shifts, r1_ref, r2_ref, weighted=True)
        return _dot(spread, slab_ref[which])

    @pl.when(step == 0)
    def _():
        for e, cp in enumerate(fetch(buf, step, 0)):
            cp.start(priority=e % 2)

    @pl.when(step + 1 < n_steps)
    def _():
        for e, cp in enumerate(fetch(1 - buf, step + 1, 0)):
            cp.start(priority=e % 2)

    for cp in fetch(buf, step, 0):
        cp.wait()
    moe = spread_dot(buf, 0)

    def extra_pass(s, acc):
        copies = fetch(buf, step, s)
        for cp in copies:
            cp.start()
        for cp in copies:
            cp.wait()
        return acc + spread_dot(buf, s)

    moe = lax.fori_loop(1, nsub_ref[step], extra_pass, moe)
    m = mod_ref[0]
    xo_ref[...] = _layer_norm(DEEPNORM_ALPHA * x_ref[...] + m[5:6] * moe, lng_ref[...], lnb_ref[...])


def _combine(off, nsub, gate, x, mod, lng, lnb, reps, ye, n_sets, n, cap, bm, seg):
    nt = n // bm
    d = x.shape[1]
    grid_spec = pltpu.PrefetchScalarGridSpec(
        num_scalar_prefetch=2,
        grid=(n_sets, nt),
        in_specs=[pl.BlockSpec((bm, LANES), lambda g, i, *_: (g * nt + i, 0)),
                  pl.BlockSpec((bm, d), lambda g, i, *_: (g * nt + i, 0)),
                  pl.BlockSpec((1, 6, d), lambda g, i, *_: (((g * nt + i) * bm) // seg, 0, 0)),
                  pl.BlockSpec((1, d), lambda g, i, *_: (0, 0)),
                  pl.BlockSpec((1, d), lambda g, i, *_: (0, 0)),
                  pl.BlockSpec(reps[0].shape, lambda g, i, *_: (0, 0)),
                  pl.BlockSpec(reps[1].shape, lambda g, i, *_: (0, 0)),
                  pl.BlockSpec(memory_space=pl.ANY)],
        out_specs=pl.BlockSpec((bm, d), lambda g, i, *_: (g * nt + i, 0)),
        scratch_shapes=[pltpu.VMEM((2, N_EXPERTS * MOE_SLAB, d), BF16),
                        pltpu.SemaphoreType.DMA((2, N_EXPERTS))],
    )
    return pl.pallas_call(
        functools.partial(_combine_kernel, nt=nt, cap=cap, n_steps=n_sets * nt),
        grid_spec=grid_spec,
        out_shape=jax.ShapeDtypeStruct(x.shape, F32),
        compiler_params=_cparams(("arbitrary", "arbitrary")),
    )(off, nsub, gate, x, mod, lng.reshape(1, d), lnb.reshape(1, d), *reps, ye)


def _rope_lanes(w):
    z = jnp.zeros(w.shape[:-1] + (ROPE_HALF,), w.dtype)
    return jnp.concatenate([w[..., :ROPE_HALF], z, w[..., ROPE_HALF:], z], axis=-1)


def _proj_weight(w_in):
    c0 = S5_WIDTH + Q_LORA + KV_LORA
    kr = w_in[:, c0:c0 + QK_ROPE]
    gates = w_in[:, c0 + QK_ROPE:]
    return jnp.concatenate([w_in[:, :c0], gates, _rope_lanes(kr)], axis=1).astype(BF16)


def _q_weight(w_uq):
    w = w_uq.reshape(Q_LORA, MLA_HEADS, QK_NOPE + QK_ROPE)
    w = jnp.concatenate([w[..., :QK_NOPE], _rope_lanes(w[..., QK_NOPE:])], axis=-1)
    return w.reshape(Q_LORA, MLA_HEADS * HEAD_PAD).astype(BF16)


def _kv_weights(w_ukv):
    w = w_ukv.reshape(KV_LORA, MLA_HEADS, QK_NOPE + V_HEAD)
    wk = w[..., :QK_NOPE].reshape(KV_LORA, MLA_HEADS * QK_NOPE)
    wv = w[..., QK_NOPE:].reshape(KV_LORA, MLA_HEADS * V_HEAD)
    return wk.astype(BF16), wv.astype(BF16)


def _rope_tables(length):
    pos = jnp.arange(length, dtype=F32)
    inv = 1.0 / (ROPE_BASE ** (jnp.arange(0, QK_ROPE, 2, dtype=F32) / QK_ROPE))
    ang = pos[:, None] * inv[None, :]
    cos, sin = jnp.cos(ang), jnp.sin(ang)
    z = jnp.zeros_like(cos)
    return (jnp.concatenate([cos, z, cos, z], axis=1), jnp.concatenate([-sin, z, sin, z], axis=1))


def _trunk(x_groups, c_groups, p):
    (b1, l1, d), (b2, l2, _) = x_groups[0].shape, x_groups[1].shape
    n1, n2 = b1 * l1, b2 * l2
    n = n1 + n2
    seg = l1
    assert l2 % l1 == 0 and n1 == n2 and n1 % l2 == 0
    cap = EC_CAPACITY * n1 // N_EXPERTS
    bm = min(512, seg)
    bm_in = min(1024, seg)
    bm_moe = min(256, seg)
    tq = min(1024, l1)
    tm = min(256, cap)
    assert cap >= MOE_SLAB and cap % tm == 0

    x = jnp.concatenate([x_groups[0].reshape(n1, d), x_groups[1].reshape(n2, d)], axis=0)
    x = _ln_in(x, p['ln_in_g'], p['ln_in_b'], bm)

    c = jnp.concatenate(list(c_groups), axis=0)
    c_rows = -(-c.shape[0] // SUBLANES) * SUBLANES
    mod_all = _ada(jnp.pad(c, ((0, c_rows - c.shape[0]), (0, 0))), p['w_ada'], p['b_ada'])
    seg_rows = jnp.array(list(range(b1)) + [b1 + b for b in range(b2) for _ in range(l2 // seg)])
    mod_all = mod_all.reshape(DEPTH, c_rows, 6, d)[:, seg_rows]

    cos, sin = _rope_tables(max(l1, l2))
    reps = _replication_matrices()

    def pos_block(bm_):
        def f(i):
            return jnp.where(i < n1 // bm_, i % (l1 // bm_), (i - n1 // bm_) % (l2 // bm_))
        return f

    for l in range(DEPTH):
        mod = mod_all[l]
        proj = _inproj(x, mod, _proj_weight(p['w_in'][l]), bm_in, PROJ_W // 7, seg)

        m_op, w_op, v_op, a_op = _s5_operators(p['s5_lam_re'][l], p['s5_lam_im'][l], p['s5_log_step'][l],
                                               p['s5_b_re'][l], p['s5_b_im'][l], p['s5_c_re'][l], p['s5_c_im'][l])
        y_s5 = _s5_core(proj, w_op, m_op, v_op, a_op, p['s5_d'][l],
                        ((b1, l1 // S5_CHUNK), (b2, l2 // S5_CHUNK)))
        z = _s5_post(y_s5, p['s5_w_glu'][l].astype(BF16), p['s5_b_glu'][l], min(bm, n // S5_CHUNK))

        q = _qproj(proj, p['mla_q_norm'][l], _q_weight(p['mla_w_uq'][l]), cos, sin, bm, pos_block(bm))
        wk, wv = _kv_weights(p['mla_w_ukv'][l])
        k, v = _kvproj(proj, p['mla_kv_norm'][l], wk, wv, cos, sin, bm, pos_block(bm))
        o1 = _attention(q, k, v, 0, b1, l1, tq)
        o2 = _attention(q, k, v, n1, b2, l2, tq)

        merged = _merge(z, o1, o2, proj, p['w_s5_proj'][l].astype(BF16), p['w_mla_proj'][l].astype(BF16), bm)
        wr = jnp.pad(p['w_router'][l], ((0, 0), (0, LANES - N_EXPERTS))).astype(BF16)
        x, h2, aff = _out_proj(merged, x, mod, p['w_o'][l].astype(BF16), p['ln1_g'][l], p['ln1_b'][l],
                               wr, bm, seg)

        gate, cnt, off = _select(aff, 2, n1, cap, bm_moe)
        cnt, off = cnt[..., :N_EXPERTS], off[..., :N_EXPERTS]
        filled = off % SLOT_ALIGN + cnt
        npass = filled // MOE_SLAB + 1
        coff = (filled // SLOT_ALIGN) * SLOT_ALIGN - (npass - 1) * MOE_SLAB
        nsub = jnp.max(npass, axis=-1).reshape(-1)
        off, npass, coff = off.reshape(-1), npass.reshape(-1), coff.reshape(-1)
        xe = _compact(off, npass, coff, nsub, h2, gate, reps, 2, n1, cap, bm_moe)
        ye = _ffn(xe, p['w_exp1'][l].astype(BF16), p['w_exp3'][l].astype(BF16),
                  p['w_exp2'][l].astype(BF16), 2, cap, tm)
        x = _combine(off, nsub, gate, x, mod, p['ln2_g'][l], p['ln2_b'][l], reps, ye, 2, n1, cap, bm_moe, seg)

    return x[:n1].reshape(b1, l1, d), x[n1:].reshape(b2, l2, d)


def kernel(x_prompt, x_sample, c_prompt, c_sample, ln_in_g, ln_in_b, w_ada, b_ada, w_in, s5_lam_re, s5_lam_im, s5_log_step, s5_b_re, s5_b_im, s5_c_re, s5_c_im, s5_d, s5_w_glu, s5_b_glu, w_s5_proj, mla_q_norm, mla_w_uq, mla_kv_norm, mla_w_ukv, w_mla_proj, w_o, ln1_g, ln1_b, w_router, w_exp1, w_exp3, w_exp2, ln2_g, ln2_b):
    params = dict(ln_in_g=ln_in_g, ln_in_b=ln_in_b, w_ada=w_ada, b_ada=b_ada, w_in=w_in,
                  s5_lam_re=s5_lam_re, s5_lam_im=s5_lam_im, s5_log_step=s5_log_step,
                  s5_b_re=s5_b_re, s5_b_im=s5_b_im, s5_c_re=s5_c_re, s5_c_im=s5_c_im, s5_d=s5_d,
                  s5_w_glu=s5_w_glu, s5_b_glu=s5_b_glu, w_s5_proj=w_s5_proj,
                  mla_q_norm=mla_q_norm, mla_w_uq=mla_w_uq, mla_kv_norm=mla_kv_norm,
                  mla_w_ukv=mla_w_ukv, w_mla_proj=w_mla_proj, w_o=w_o, ln1_g=ln1_g, ln1_b=ln1_b,
                  w_router=w_router, w_exp1=w_exp1, w_exp3=w_exp3, w_exp2=w_exp2,
                  ln2_g=ln2_g, ln2_b=ln2_b)
    return _trunk((x_prompt, x_sample), (c_prompt, c_sample), params)
```
